```python
import jax, jax.numpy as jnp
from jax import lax
import numpy as np

D_MODEL = 2048
BATCH = 4
SEQ = 2048
DEPTH = 4
DEC_BATCH = 1
DEC_SEQ = 16384
PAST_LEN = 128

N_MIXERS = 2
N_POOL_LAYERS = (DEPTH + 1) // 2
N_ATTN_LAYERS = DEPTH // 2
POOL_WINDOWS = (2, 4, 8, 16)
N_POOL_GROUPS = len(POOL_WINDOWS)
POOL_GROUP = D_MODEL // N_POOL_GROUPS
HEAD_DIM = 64
N_HEADS = D_MODEL // HEAD_DIM
N_KV_HEADS = N_HEADS // 8
GQA_GROUP = N_HEADS // N_KV_HEADS
WINDOW = 128
BLOCK = 128
D_FF = 5632
CONV_WIDTH = 3
EPS = 1e-6

kernel_name = "hybrid_pool_swa_convffn_encoder"


def rmsnorm(x, g):
    xf = x.astype(jnp.float32)
    y = xf * lax.rsqrt(jnp.mean(xf * xf, axis=-1, keepdims=True) + EPS)
    return (y * g.astype(jnp.float32)).astype(x.dtype)


def alibi_slopes():
    h = np.arange(1, N_HEADS + 1, dtype=np.float32)
    return jnp.asarray(2.0 ** (-8.0 * h / N_HEADS), dtype=jnp.float32)


def pool_mixer(h, w, scale):
    B, S, D = h.shape
    hf = h.astype(jnp.float32)
    cs = jnp.concatenate([jnp.zeros((B, 1, D), jnp.float32), jnp.cumsum(hf, axis=1)], axis=1)
    t = jnp.arange(S)
    outs = []
    for g, win in enumerate(POOL_WINDOWS):
        lo = jnp.clip(t - win // 2, 0, S)
        hi = jnp.clip(t + win // 2, 0, S)
        sl = slice(g * POOL_GROUP, (g + 1) * POOL_GROUP)
        csg = cs[:, :, sl]
        cnt = (hi - lo).astype(jnp.float32)[None, :, None]
        mean = (csg[:, hi] - csg[:, lo]) / cnt
        outs.append(mean - hf[:, :, sl])
    p = jnp.stack(outs, axis=2).astype(h.dtype)
    y = jnp.einsum('bsgc,gcd->bsgd', p, w).reshape(B, S, D)
    return y * scale


def window_attention(h, w_qkv, w_o, sink):
    B, S, D = h.shape
    nb = S // BLOCK
    qkv = h @ w_qkv
    q, k, v = jnp.split(qkv, [N_HEADS * HEAD_DIM, (N_HEADS + N_KV_HEADS) * HEAD_DIM], axis=-1)
    q = q.reshape(B, nb, BLOCK, N_KV_HEADS, GQA_GROUP, HEAD_DIM)

    def band(z):
        z = z.reshape(B, S, N_KV_HEADS, HEAD_DIM)
        z = jnp.pad(z, ((0, 0), (BLOCK, BLOCK), (0, 0), (0, 0)))
        z = z.reshape(B, nb + 2, BLOCK, N_KV_HEADS, HEAD_DIM)
        return jnp.concatenate([z[:, :-2], z[:, 1:-1], z[:, 2:]], axis=2)

    kb, vb = band(k), band(v)
    s = jnp.einsum('bnqkgd,bnckd->bnkgqc', q, kb).astype(jnp.float32) * (HEAD_DIM ** -0.5)

    qi = jnp.arange(BLOCK)[:, None]
    kc = jnp.arange(3 * BLOCK)[None, :]
    rel = kc - BLOCK - qi
    kpos = jnp.arange(nb)[:, None] * BLOCK + jnp.arange(3 * BLOCK)[None, :] - BLOCK
    valid = (jnp.abs(rel) <= WINDOW)[None] & ((kpos >= 0) & (kpos < S))[:, None, :]
    bias = -alibi_slopes()[:, None, None] * jnp.abs(rel).astype(jnp.float32)[None]
    bias = bias.reshape(N_KV_HEADS, GQA_GROUP, BLOCK, 3 * BLOCK)
    s = jnp.where(valid[None, :, None, None], s + bias[None, None], -jnp.inf)

    sk = sink.astype(jnp.float32).reshape(1, 1, N_KV_HEADS, GQA_GROUP, 1, 1)
    m = jnp.maximum(jnp.max(s, axis=-1, keepdims=True), sk)
    e = jnp.exp(s - m)
    p = e / (jnp.sum(e, axis=-1, keepdims=True) + jnp.exp(sk - m))
    o = jnp.einsum('bnkgqc,bnckd->bnqkgd', p.astype(vb.dtype), vb).reshape(B, S, N_HEADS * HEAD_DIM)
    return o @ w_o


def conv_ffn(h, w_up, conv_w, conv_b, w_down):
    u = h @ w_up
    up = jnp.pad(u, ((0, 0), (1, 1), (0, 0)))
    u = up[:, :-2] * conv_w[0] + up[:, 1:-1] * conv_w[1] + up[:, 2:] * conv_w[2] + conv_b
    g, val = jnp.split(u, 2, axis=-1)
    return (jax.nn.silu(g) * val) @ w_down


def trunk(x, norm_mix, norm_ffn, norm_final, pool_w, pool_scale, attn_wqkv, attn_wo, attn_sink,
          ffn_wup, ffn_conv_w, ffn_conv_b, ffn_wdown):
    for i in range(DEPTH):
        h = rmsnorm(x, norm_mix[i])
        j = i // N_MIXERS
        if i % N_MIXERS == 0:
            x = x + pool_mixer(h, pool_w[j], pool_scale[j])
        else:
            x = x + window_attention(h, attn_wqkv[j], attn_wo[j], attn_sink[j])
        x = x + conv_ffn(rmsnorm(x, norm_ffn[i]), ffn_wup[i], ffn_conv_w[i], ffn_conv_b[i], ffn_wdown[i])
    return rmsnorm(x, norm_final)


def setup_inputs(seed: int = 0) -> dict:
    key = jax.random.key(seed)
    ks = jax.random.split(key, 16)
    f32 = jnp.float32
    nrm = lambda k, shape, s: jax.random.normal(k, shape, f32) * s
    QKV = (N_HEADS + 2 * N_KV_HEADS) * HEAD_DIM
    center = jnp.zeros((CONV_WIDTH, 1), f32).at[CONV_WIDTH // 2].set(1.0)
    return {
        "x_prompt": nrm(ks[0], (BATCH, SEQ, D_MODEL), 1.0),
        "x_sample": nrm(ks[1], (DEC_BATCH, DEC_SEQ, D_MODEL), 1.0),
        "norm_mix": 1.0 + nrm(ks[2], (DEPTH, D_MODEL), 0.02),
        "norm_ffn": 1.0 + nrm(ks[3], (DEPTH, D_MODEL), 0.02),
        "norm_final": 1.0 + nrm(ks[4], (D_MODEL,), 0.02),
        "pool_w": nrm(ks[5], (N_POOL_LAYERS, N_POOL_GROUPS, POOL_GROUP, POOL_GROUP), POOL_GROUP ** -0.5),
        "pool_scale": 1.0 + nrm(ks[6], (N_POOL_LAYERS, D_MODEL), 0.02),
        "attn_wqkv": nrm(ks[7], (N_ATTN_LAYERS, D_MODEL, QKV), D_MODEL ** -0.5),
        "attn_wo": nrm(ks[8], (N_ATTN_LAYERS, N_HEADS * HEAD_DIM, D_MODEL), (N_HEADS * HEAD_DIM) ** -0.5),
        "attn_sink": nrm(ks[9], (N_ATTN_LAYERS, N_HEADS), 0.5),
        "ffn_wup": nrm(ks[10], (DEPTH, D_MODEL, 2 * D_FF), D_MODEL ** -0.5),
        "ffn_conv_w": center[None] + nrm(ks[11], (DEPTH, CONV_WIDTH, 2 * D_FF), 0.2),
        "ffn_conv_b": nrm(ks[12], (DEPTH, 2 * D_FF), 0.01),
        "ffn_wdown": nrm(ks[13], (DEPTH, D_FF, D_MODEL), D_FF ** -0.5),
    }


def reference(x_prompt, x_sample, norm_mix, norm_ffn, norm_final, pool_w, pool_scale, attn_wqkv, attn_wo,
              attn_sink, ffn_wup, ffn_conv_w, ffn_conv_b, ffn_wdown):
    y_prompt = trunk(x_prompt, norm_mix, norm_ffn, norm_final, pool_w, pool_scale, attn_wqkv, attn_wo,
                     attn_sink, ffn_wup, ffn_conv_w, ffn_conv_b, ffn_wdown)
    y_sample = trunk(x_sample, norm_mix, norm_ffn, norm_final, pool_w, pool_scale, attn_wqkv, attn_wo,
                     attn_sink, ffn_wup, ffn_conv_w, ffn_conv_b, ffn_wdown)
    return (y_prompt, y_sample)
```

```python
import functools

import numpy as np
import jax
import jax.numpy as jnp
from jax import lax
from jax.experimental import pallas as pl
from jax.experimental.pallas import tpu as pltpu

D_MODEL = 2048
DEPTH = 4
POOL_WINDOWS = (2, 4, 8, 16)
POOL_GROUP = D_MODEL // len(POOL_WINDOWS)
HEAD_DIM = 64
N_HEADS = D_MODEL // HEAD_DIM
N_KV_HEADS = N_HEADS // 8
HEADS_PER_KV = N_HEADS // N_KV_HEADS
WINDOW = 128
BLOCK = 128
D_FF = 5632
EPS = 1e-6

F32 = jnp.float32
BF16 = jnp.bfloat16

PAIR_LANES = 2 * HEAD_DIM
KV_DUP_WIDTH = N_KV_HEADS * PAIR_LANES
QKV_EXT_WIDTH = D_MODEL + 2 * KV_DUP_WIDTH

VMEM_LIMIT_BYTES = 56 * 1024 * 1024

FFN_TM = 512
FFN_TF = 512
FFN_HALO = 16
POOL_TM = 512
POOL_HALO = 8
QKV_TM = 1024
QKV_TN = 1024


def _rms(x, g):
    ms = jnp.mean(x * x, axis=-1, keepdims=True)
    return x * lax.rsqrt(ms + EPS) * g


def _dot(a, b):
    return jnp.dot(a, b, preferred_element_type=F32)


def _ffn_kernel(x_ref, xp_ref, xn_ref, g_ref, wug_ref, wuv_ref, cwg_ref, cwv_ref, cbg_ref, cbv_ref,
                wd_ref, gf_ref, o_ref, hb_ref, *, seq, tm, final_norm):
    i = pl.program_id(0)
    j = pl.program_id(1)
    halo = FFN_HALO

    @pl.when(j == 0)
    def _():
        g = g_ref[...]
        pos0 = (i * tm) % seq
        x = x_ref[...]
        hb_ref[pl.ds(halo, tm), :] = _rms(x, g).astype(BF16)
        hp = jnp.where(pos0 == 0, 0.0, _rms(xp_ref[...], g))
        hb_ref[pl.ds(0, halo), :] = hp.astype(BF16)
        hn = jnp.where(pos0 + tm == seq, 0.0, _rms(xn_ref[...], g))
        hb_ref[pl.ds(halo + tm, halo), :] = hn.astype(BF16)
        o_ref[...] = x

    hb = hb_ref[...]

    def conv(u, cw_ref, cb_ref):
        cw = cw_ref[...]
        return (u[halo - 1:halo - 1 + tm] * cw[0:1] + u[halo:halo + tm] * cw[1:2]
                + u[halo + 1:halo + 1 + tm] * cw[2:3] + cb_ref[...])

    gate = conv(_dot(hb, wug_ref[...]), cwg_ref, cbg_ref)
    val = conv(_dot(hb, wuv_ref[...]), cwv_ref, cbv_ref)
    act = gate / (1.0 + jnp.exp(-gate)) * val
    o_ref[...] += _dot(act.astype(BF16), wd_ref[...])

    if final_norm:
        @pl.when(j == pl.num_programs(1) - 1)
        def _():
            o_ref[...] = _rms(o_ref[...], gf_ref[...])


def _ffn(x, seq, g, wup, cw, cb, wdown, g_final, final_norm):
    n = x.shape[0]
    tm, tf, halo = FFN_TM, FFN_TF, FFN_HALO
    nf = D_FF // tf
    hpt = tm // halo
    last_halo_block = n // halo - 1
    kern = functools.partial(_ffn_kernel, seq=seq, tm=tm, final_norm=final_norm)
    return pl.pallas_call(
        kern,
        grid=(n // tm, nf),
        in_specs=[
            pl.BlockSpec((tm, D_MODEL), lambda i, j: (i, 0)),
            pl.BlockSpec((halo, D_MODEL), lambda i, j: (jnp.maximum(i * hpt - 1, 0), 0)),
            pl.BlockSpec((halo, D_MODEL), lambda i, j: (jnp.minimum((i + 1) * hpt, last_halo_block), 0)),
            pl.BlockSpec((1, D_MODEL), lambda i, j: (0, 0)),
            pl.BlockSpec((D_MODEL, tf), lambda i, j: (0, j)),
            pl.BlockSpec((D_MODEL, tf), lambda i, j: (0, nf + j)),
            pl.BlockSpec((3, tf), lambda i, j: (0, j)),
            pl.BlockSpec((3, tf), lambda i, j: (0, nf + j)),
            pl.BlockSpec((1, tf), lambda i, j: (0, j)),
            pl.BlockSpec((1, tf), lambda i, j: (0, nf + j)),
            pl.BlockSpec((tf, D_MODEL), lambda i, j: (j, 0)),
            pl.BlockSpec((1, D_MODEL), lambda i, j: (0, 0)),
        ],
        out_specs=pl.BlockSpec((tm, D_MODEL), lambda i, j: (i, 0)),
        out_shape=jax.ShapeDtypeStruct((n, D_MODEL), F32),
        scratch_shapes=[pltpu.VMEM((tm + 2 * halo, D_MODEL), BF16)],
        compiler_params=pltpu.CompilerParams(
            dimension_semantics=("arbitrary", "arbitrary"), vmem_limit_bytes=VMEM_LIMIT_BYTES),
        name="conv_ffn",
    )(x, x, x, g, wup, wup, cw, cw, cb, cb, wdown, g_final)


def _pool_kernel(x_ref, xp_ref, xn_ref, g_ref, w_ref, sc_ref, o_ref, hh_ref, *, seq, tm):
    i = pl.program_id(0)
    halo = POOL_HALO
    pos0 = (i * tm) % seq
    g = g_ref[...]
    x = x_ref[...]
    h = _rms(x, g)
    hh_ref[pl.ds(halo, tm), :] = h
    hh_ref[pl.ds(0, halo), :] = jnp.where(pos0 == 0, 0.0, _rms(xp_ref[...], g))
    hh_ref[pl.ds(halo + tm, halo), :] = jnp.where(pos0 + tm == seq, 0.0, _rms(xn_ref[...], g))
    t = pos0 + lax.broadcasted_iota(jnp.int32, (tm, 1), 0)
    for gi, win in enumerate(POOL_WINDOWS):
        half = win // 2
        lanes = pl.ds(gi * POOL_GROUP, POOL_GROUP)
        acc = hh_ref[pl.ds(halo - half, tm), lanes]
        for d in range(1, win):
            acc = acc + hh_ref[pl.ds(halo - half + d, tm), lanes]
        cnt = (jnp.clip(t + half, 0, seq) - jnp.clip(t - half, 0, seq)).astype(F32)
        p = acc / cnt - hh_ref[pl.ds(halo, tm), lanes]
        y = _dot(p.astype(BF16), w_ref[gi]) * sc_ref[:, lanes]
        o_ref[:, lanes] = x_ref[:, lanes] + y


def _pool(x, seq, g, w, scale):
    n = x.shape[0]
    tm, halo = POOL_TM, POOL_HALO
    hpt = tm // halo
    last_halo_block = n // halo - 1
    kern = functools.partial(_pool_kernel, seq=seq, tm=tm)
    return pl.pallas_call(
        kern,
        grid=(n // tm,),
        in_specs=[
            pl.BlockSpec((tm, D_MODEL), lambda i: (i, 0)),
            pl.BlockSpec((halo, D_MODEL), lambda i: (jnp.maximum(i * hpt - 1, 0), 0)),
            pl.BlockSpec((halo, D_MODEL), lambda i: (jnp.minimum((i + 1) * hpt, last_halo_block), 0)),
            pl.BlockSpec((1, D_MODEL), lambda i: (0, 0)),
            pl.BlockSpec((len(POOL_WINDOWS), POOL_GROUP, POOL_GROUP), lambda i: (0, 0, 0)),
            pl.BlockSpec((1, D_MODEL), lambda i: (0, 0)),
        ],
        out_specs=pl.BlockSpec((tm, D_MODEL), lambda i: (i, 0)),
        out_shape=jax.ShapeDtypeStruct((n, D_MODEL), F32),
        scratch_shapes=[pltpu.VMEM((tm + 2 * halo, D_MODEL), F32)],
        compiler_params=pltpu.CompilerParams(
            dimension_semantics=("arbitrary",), vmem_limit_bytes=VMEM_LIMIT_BYTES),
        name="pool_mixer",
    )(x, x, x, g, w, scale)


def _qkv_kernel(x_ref, g_ref, w_ref, o_ref, hb_ref):
    @pl.when(pl.program_id(1) == 0)
    def _():
        hb_ref[...] = _rms(x_ref[...], g_ref[...]).astype(BF16)

    o_ref[...] = _dot(hb_ref[...], w_ref[...]).astype(BF16)


def _qkv(x, g, w_ext):
    n = x.shape[0]
    tm, tn = QKV_TM, QKV_TN
    return pl.pallas_call(
        _qkv_kernel,
        grid=(n // tm, QKV_EXT_WIDTH // tn),
        in_specs=[
            pl.BlockSpec((tm, D_MODEL), lambda i, j: (i, 0)),
            pl.BlockSpec((1, D_MODEL), lambda i, j: (0, 0)),
            pl.BlockSpec((D_MODEL, tn), lambda i, j: (0, j)),
        ],
        out_specs=pl.BlockSpec((tm, tn), lambda i, j: (i, j)),
        out_shape=jax.ShapeDtypeStruct((n, QKV_EXT_WIDTH), BF16),
        scratch_shapes=[pltpu.VMEM((tm, D_MODEL), BF16)],
        compiler_params=pltpu.CompilerParams(
            dimension_semantics=("arbitrary", "arbitrary"), vmem_limit_bytes=VMEM_LIMIT_BYTES),
        name="qkv_proj",
    )(x, g, w_ext)


def _attn_kernel(sink_ref, q_ref, kp_ref, kc_ref, kn_ref, vp_ref, vc_ref, vn_ref, bias_ref, wo_ref, x_ref,
                 o_ref, kb_ref, vb_ref, ob_ref, *, seq):
    i = pl.program_id(0)
    pos0 = (i * BLOCK) % seq
    kb_ref[pl.ds(0, BLOCK), :] = kp_ref[...]
    kb_ref[pl.ds(BLOCK, BLOCK), :] = kc_ref[...]
    kb_ref[pl.ds(2 * BLOCK, BLOCK), :] = kn_ref[...]
    vb_ref[pl.ds(0, BLOCK), :] = vp_ref[...]
    vb_ref[pl.ds(BLOCK, BLOCK), :] = vc_ref[...]
    vb_ref[pl.ds(2 * BLOCK, BLOCK), :] = vn_ref[...]

    col = lax.broadcasted_iota(jnp.int32, (1, 3 * BLOCK), 1)
    lo = jnp.where(pos0 == 0, BLOCK, 0)
    hi = jnp.where(pos0 + BLOCK == seq, 2 * BLOCK, 3 * BLOCK)
    pen = jnp.where((col >= lo) & (col < hi), 0.0, -jnp.inf)

    lane = lax.broadcasted_iota(jnp.int32, (3 * BLOCK, PAIR_LANES), 1)
    low_half = lane < HEAD_DIM
    zero = jnp.zeros((3 * BLOCK, PAIR_LANES), BF16)
    scale = HEAD_DIM ** -0.5
    for kv in range(N_KV_HEADS):
        kk = kb_ref[:, pl.ds(kv * PAIR_LANES, PAIR_LANES)]
        vv = vb_ref[:, pl.ds(kv * PAIR_LANES, PAIR_LANES)]
        k_half = (jnp.where(low_half, kk, zero), jnp.where(low_half, zero, kk))
        v_half = (jnp.where(low_half, vv, zero), jnp.where(low_half, zero, vv))
        for pr in range(HEADS_PER_KV // 2):
            pair = kv * (HEADS_PER_KV // 2) + pr
            qp = q_ref[:, pl.ds(pair * PAIR_LANES, PAIR_LANES)]
            o_pair = None
            for par in range(2):
                head = 2 * pair + par
                s = lax.dot_general(qp, k_half[par], (((1,), (1,)), ((), ())), preferred_element_type=F32)
                s = s * scale + bias_ref[head] + pen
                sk = sink_ref[head]
                m = jnp.maximum(jnp.max(s, axis=-1, keepdims=True), sk)
                e = jnp.exp(s - m)
                denom = jnp.sum(e, axis=-1, keepdims=True) + jnp.exp(sk - m)
                o = _dot(e.astype(BF16), v_half[par]) / denom
                o_pair = o if o_pair is None else o_pair + o
            ob_ref[:, pl.ds(pair * PAIR_LANES, PAIR_LANES)] = o_pair.astype(BF16)
    o_ref[...] = x_ref[...] + _dot(ob_ref[...], wo_ref[...])


def _attn(x, qkv, seq, bias, wo, sink):
    n = x.shape[0]
    nblk = n // BLOCK
    kcol = D_MODEL // KV_DUP_WIDTH
    vcol = kcol + 1
    prev = lambda i: jnp.maximum(i - 1, 0)
    nxt = lambda i: jnp.minimum(i + 1, nblk - 1)
    kern = functools.partial(_attn_kernel, seq=seq)
    return pl.pallas_call(
        kern,
        grid=(nblk,),
        in_specs=[
            pl.BlockSpec(memory_space=pltpu.SMEM),
            pl.BlockSpec((BLOCK, D_MODEL), lambda i: (i, 0)),
            pl.BlockSpec((BLOCK, KV_DUP_WIDTH), lambda i: (prev(i), kcol)),
            pl.BlockSpec((BLOCK, KV_DUP_WIDTH), lambda i: (i, kcol)),
            pl.BlockSpec((BLOCK, KV_DUP_WIDTH), lambda i: (nxt(i), kcol)),
            pl.BlockSpec((BLOCK, KV_DUP_WIDTH), lambda i: (prev(i), vcol)),
            pl.BlockSpec((BLOCK, KV_DUP_WIDTH), lambda i: (i, vcol)),
            pl.BlockSpec((BLOCK, KV_DUP_WIDTH), lambda i: (nxt(i), vcol)),
            pl.BlockSpec((N_HEADS, BLOCK, 3 * BLOCK), lambda i: (0, 0, 0)),
            pl.BlockSpec((D_MODEL, D_MODEL), lambda i: (0, 0)),
            pl.BlockSpec((BLOCK, D_MODEL), lambda i: (i, 0)),
        ],
        out_specs=pl.BlockSpec((BLOCK, D_MODEL), lambda i: (i, 0)),
        out_shape=jax.ShapeDtypeStruct((n, D_MODEL), F32),
        scratch_shapes=[
            pltpu.VMEM((3 * BLOCK, KV_DUP_WIDTH), BF16),
            pltpu.VMEM((3 * BLOCK, KV_DUP_WIDTH), BF16),
            pltpu.VMEM((BLOCK, D_MODEL), BF16),
        ],
        compiler_params=pltpu.CompilerParams(
            dimension_semantics=("arbitrary",), vmem_limit_bytes=VMEM_LIMIT_BYTES),
        name="window_attn",
    )(sink, qkv, qkv, qkv, qkv, qkv, qkv, qkv, bias, wo, x)


def _alibi_bias():
    h = np.arange(1, N_HEADS + 1, dtype=np.float32)
    slopes = jnp.asarray(2.0 ** (-8.0 * h / N_HEADS), dtype=F32)
    qi = jnp.arange(BLOCK)[:, None]
    kc = jnp.arange(3 * BLOCK)[None, :]
    dist = jnp.abs(kc - BLOCK - qi)
    bias = -slopes[:, None, None] * dist.astype(F32)[None]
    return jnp.where((dist <= WINDOW)[None], bias, -jnp.inf)


def _extend_qkv_weight(w_qkv):
    kv_width = N_KV_HEADS * HEAD_DIM
    wq = w_qkv[:, :D_MODEL]
    wk = w_qkv[:, D_MODEL:D_MODEL + kv_width]
    wv = w_qkv[:, D_MODEL + kv_width:]

    def dup(w):
        w = w.reshape(D_MODEL, N_KV_HEADS, 1, HEAD_DIM)
        return jnp.broadcast_to(w, (D_MODEL, N_KV_HEADS, 2, HEAD_DIM)).reshape(D_MODEL, KV_DUP_WIDTH)

    return jnp.concatenate([wq, dup(wk), dup(wv)], axis=1).astype(BF16)


def _trunk(x3, params):
    b, s, d = x3.shape
    x = x3.reshape(b * s, d)
    for i in range(DEPTH):
        jm = i // 2
        g_mix = params["norm_mix"][i][None]
        if i % 2 == 0:
            x = _pool(x, s, g_mix, params["pool_w"][jm], params["pool_scale"][jm][None])
        else:
            qkv = _qkv(x, g_mix, params["wqkv_ext"][jm])
            x = _attn(x, qkv, s, params["bias"], params["attn_wo"][jm], params["attn_sink"][jm])
        x = _ffn(x, s, params["norm_ffn"][i][None], params["ffn_wup"][i], params["ffn_conv_w"][i],
                 params["ffn_conv_b"][i][None], params["ffn_wdown"][i], params["norm_final"][None],
                 final_norm=(i == DEPTH - 1))
    return x.reshape(b, s, d)


def kernel(x_prompt, x_sample, norm_mix, norm_ffn, norm_final, pool_w, pool_scale, attn_wqkv, attn_wo,
           attn_sink, ffn_wup, ffn_conv_w, ffn_conv_b, ffn_wdown):
    params = {
        "norm_mix": norm_mix,
        "norm_ffn": norm_ffn,
        "norm_final": norm_final,
        "pool_w": pool_w.astype(BF16),
        "pool_scale": pool_scale,
        "wqkv_ext": [_extend_qkv_weight(attn_wqkv[j]) for j in range(attn_wqkv.shape[0])],
        "attn_wo": attn_wo.astype(BF16),
        "attn_sink": attn_sink,
        "bias": _alibi_bias(),
        "ffn_wup": ffn_wup.astype(BF16),
        "ffn_conv_w": ffn_conv_w,
        "ffn_conv_b": ffn_conv_b,
        "ffn_wdown": ffn_wdown.astype(BF16),
    }
    return (_trunk(x_prompt, params), _trunk(x_sample, params))
```

```python
import functools

import numpy as np
import jax
import jax.numpy as jnp
from jax import lax
from jax.experimental import pallas as pl
from jax.experimental.pallas import tpu as pltpu

D_MODEL = 2048
DEPTH = 4
POOL_WINDOWS = (2, 4, 8, 16)
POOL_GROUP = D_MODEL // len(POOL_WINDOWS)
HEAD_DIM = 64
N_HEADS = D_MODEL // HEAD_DIM
N_KV_HEADS = N_HEADS // 8
HEADS_PER_KV = N_HEADS // N_KV_HEADS
WINDOW = 128
BLOCK = 128
D_FF = 5632
EPS = 1e-6
LOG2E = 1.4426950408889634

F32 = jnp.float32
BF16 = jnp.bfloat16

SUBLANES = 8
LANES = 128
BF16_ROWS = 16

PAIR_LANES = 2 * HEAD_DIM
KV_DUP_WIDTH = N_KV_HEADS * PAIR_LANES
QKV_EXT_WIDTH = D_MODEL + 2 * KV_DUP_WIDTH

VMEM_LIMIT_BYTES = 56 * 1024 * 1024

FFN_TM = 512
FFN_TF = 512
FFN_EDGE = BF16_ROWS
POOL_TM = 512
POOL_HALO = SUBLANES
QKV_TM = 1024
QKV_TN = 1024


def _rms(x, g):
    ms = jnp.mean(x * x, axis=-1, keepdims=True)
    return x * lax.rsqrt(ms + EPS) * g


def _dot(a, b):
    return jnp.dot(a, b, preferred_element_type=F32)


def _ffn_kernel(x_ref, xp_ref, xn_ref, g_ref, wu_ref, cw_ref, cb_ref, wd_ref, gf_ref, o_ref, hb_ref, acc_ref,
                *, seq, tm, tf, final_norm):
    i = pl.program_id(0)
    j = pl.program_id(1)
    nv = tm // SUBLANES

    @pl.when(j == 0)
    def _():
        g = g_ref[...]
        pos0 = (i * tm) % seq
        xperm = jnp.swapaxes(x_ref[...].reshape(SUBLANES, nv, D_MODEL), 0, 1).reshape(tm, D_MODEL)
        hb_ref[pl.ds(0, tm), :] = _rms(xperm, g).astype(BF16)
        acc_ref[...] = xperm
        hp = jnp.where(pos0 == 0, 0.0, _rms(xp_ref[...], g))[POOL_HALO - 1:POOL_HALO]
        hn = jnp.where(pos0 + tm == seq, 0.0, _rms(xn_ref[...], g))[0:1]
        edge = jnp.concatenate([hn, hp, jnp.zeros((FFN_EDGE - 2, D_MODEL), F32)], axis=0)
        hb_ref[pl.ds(tm, FFN_EDGE), :] = edge.astype(BF16)

    u = _dot(hb_ref[...], wu_ref[...])
    sub = lax.broadcasted_iota(jnp.int32, (SUBLANES, 2 * tf), 0)
    first_prev = jnp.where(sub == 0, u[tm + 1:tm + 2], pltpu.roll(u[tm - SUBLANES:tm], 1, 0))
    last_next = jnp.where(sub == SUBLANES - 1, u[tm:tm + 1], pltpu.roll(u[0:SUBLANES], SUBLANES - 1, 0))
    u_prev = jnp.concatenate([first_prev, u[:tm - SUBLANES]], axis=0)
    u_next = jnp.concatenate([u[SUBLANES:tm], last_next], axis=0)
    cw = cw_ref[...]
    conv = u_prev * cw[0:1] + u[:tm] * cw[1:2] + u_next * cw[2:3] + cb_ref[...]
    gate = conv[:, :tf]
    val = conv[:, tf:]
    act = gate / (1.0 + jnp.exp(-gate)) * val
    acc_ref[...] += _dot(act.astype(BF16), wd_ref[...])

    @pl.when(j == pl.num_programs(1) - 1)
    def _():
        out = jnp.swapaxes(acc_ref[...].reshape(nv, SUBLANES, D_MODEL), 0, 1).reshape(tm, D_MODEL)
        if final_norm:
            out = _rms(out, gf_ref[...])
        o_ref[...] = out


def _ffn(x, seq, g, wup, cw, cb, wdown, g_final, final_norm):
    n = x.shape[0]
    tm, tf, halo = FFN_TM, FFN_TF, POOL_HALO
    nf = D_FF // tf
    hpt = tm // halo
    last_halo_block = n // halo - 1
    kern = functools.partial(_ffn_kernel, seq=seq, tm=tm, tf=tf, final_norm=final_norm)
    return pl.pallas_call(
        kern,
        grid=(n // tm, nf),
        in_specs=[
            pl.BlockSpec((tm, D_MODEL), lambda i, j: (i, 0)),
            pl.BlockSpec((halo, D_MODEL), lambda i, j: (jnp.maximum(i * hpt - 1, 0), 0)),
            pl.BlockSpec((halo, D_MODEL), lambda i, j: (jnp.minimum((i + 1) * hpt, last_halo_block), 0)),
            pl.BlockSpec((1, D_MODEL), lambda i, j: (0, 0)),
            pl.BlockSpec((D_MODEL, 2 * tf), lambda i, j: (0, j)),
            pl.BlockSpec((3, 2 * tf), lambda i, j: (0, j)),
            pl.BlockSpec((1, 2 * tf), lambda i, j: (0, j)),
            pl.BlockSpec((tf, D_MODEL), lambda i, j: (j, 0)),
            pl.BlockSpec((1, D_MODEL), lambda i, j: (0, 0)),
        ],
        out_specs=pl.BlockSpec((tm, D_MODEL), lambda i, j: (i, 0)),
        out_shape=jax.ShapeDtypeStruct((n, D_MODEL), F32),
        scratch_shapes=[
            pltpu.VMEM((tm + FFN_EDGE, D_MODEL), BF16),
            pltpu.VMEM((tm, D_MODEL), F32),
        ],
        compiler_params=pltpu.CompilerParams(
            dimension_semantics=("arbitrary", "arbitrary"), vmem_limit_bytes=VMEM_LIMIT_BYTES),
        name="conv_ffn",
    )(x, x, x, g, wup, cw, cb, wdown, g_final)


def _interleave_gate_value(a):
    lead = a.shape[:-1]
    a = a.reshape(*lead, 2, D_FF // FFN_TF, FFN_TF)
    return jnp.swapaxes(a, -3, -2).reshape(*lead, 2 * D_FF)


def _pool_kernel(x_ref, xp_ref, xn_ref, g_ref, w_ref, sc_ref, o_ref, hh_ref, *, seq, tm):
    i = pl.program_id(0)
    halo = POOL_HALO
    pos0 = (i * tm) % seq
    g = g_ref[...]
    x = x_ref[...]
    h = _rms(x, g)
    hh_ref[pl.ds(halo, tm), :] = h
    hh_ref[pl.ds(0, halo), :] = jnp.where(pos0 == 0, 0.0, _rms(xp_ref[...], g))
    hh_ref[pl.ds(halo + tm, halo), :] = jnp.where(pos0 + tm == seq, 0.0, _rms(xn_ref[...], g))
    t = pos0 + lax.broadcasted_iota(jnp.int32, (tm, 1), 0)
    for gi, win in enumerate(POOL_WINDOWS):
        half = win // 2
        lanes = pl.ds(gi * POOL_GROUP, POOL_GROUP)
        acc = hh_ref[pl.ds(halo - half, tm), lanes]
        for d in range(1, win):
            acc = acc + hh_ref[pl.ds(halo - half + d, tm), lanes]
        cnt = (jnp.clip(t + half, 0, seq) - jnp.clip(t - half, 0, seq)).astype(F32)
        p = acc / cnt - hh_ref[pl.ds(halo, tm), lanes]
        y = _dot(p.astype(BF16), w_ref[gi]) * sc_ref[:, lanes]
        o_ref[:, lanes] = x_ref[:, lanes] + y


def _pool(x, seq, g, w, scale):
    n = x.shape[0]
    tm, halo = POOL_TM, POOL_HALO
    hpt = tm // halo
    last_halo_block = n // halo - 1
    kern = functools.partial(_pool_kernel, seq=seq, tm=tm)
    return pl.pallas_call(
        kern,
        grid=(n // tm,),
        in_specs=[
            pl.BlockSpec((tm, D_MODEL), lambda i: (i, 0)),
            pl.BlockSpec((halo, D_MODEL), lambda i: (jnp.maximum(i * hpt - 1, 0), 0)),
            pl.BlockSpec((halo, D_MODEL), lambda i: (jnp.minimum((i + 1) * hpt, last_halo_block), 0)),
            pl.BlockSpec((1, D_MODEL), lambda i: (0, 0)),
            pl.BlockSpec((len(POOL_WINDOWS), POOL_GROUP, POOL_GROUP), lambda i: (0, 0, 0)),
            pl.BlockSpec((1, D_MODEL), lambda i: (0, 0)),
        ],
        out_specs=pl.BlockSpec((tm, D_MODEL), lambda i: (i, 0)),
        out_shape=jax.ShapeDtypeStruct((n, D_MODEL), F32),
        scratch_shapes=[pltpu.VMEM((tm + 2 * halo, D_MODEL), F32)],
        compiler_params=pltpu.CompilerParams(
            dimension_semantics=("arbitrary",), vmem_limit_bytes=VMEM_LIMIT_BYTES),
        name="pool_mixer",
    )(x, x, x, g, w, scale)


def _qkv_kernel(x_ref, g_ref, w_ref, o_ref, hb_ref):
    @pl.when(pl.program_id(1) == 0)
    def _():
        hb_ref[...] = _rms(x_ref[...], g_ref[...]).astype(BF16)

    o_ref[...] = _dot(hb_ref[...], w_ref[...]).astype(BF16)


def _qkv(x, g, w_ext):
    n = x.shape[0]
    tm, tn = QKV_TM, QKV_TN
    return pl.pallas_call(
        _qkv_kernel,
        grid=(n // tm, QKV_EXT_WIDTH // tn),
        in_specs=[
            pl.BlockSpec((tm, D_MODEL), lambda i, j: (i, 0)),
            pl.BlockSpec((1, D_MODEL), lambda i, j: (0, 0)),
            pl.BlockSpec((D_MODEL, tn), lambda i, j: (0, j)),
        ],
        out_specs=pl.BlockSpec((tm, tn), lambda i, j: (i, j)),
        out_shape=jax.ShapeDtypeStruct((n, QKV_EXT_WIDTH), BF16),
        scratch_shapes=[pltpu.VMEM((tm, D_MODEL), BF16)],
        compiler_params=pltpu.CompilerParams(
            dimension_semantics=("arbitrary", "arbitrary"), vmem_limit_bytes=VMEM_LIMIT_BYTES),
        name="qkv_proj",
    )(x, g, w_ext)


def _attn_kernel(sink_ref, q_ref, kp_ref, kc_ref, kn_ref, vp_ref, vc_ref, vn_ref, bias_ref, wo_ref, x_ref,
                 o_ref, kb_ref, vb_ref, ob_ref, s_ref, e_ref, rd_ref, *, seq):
    i = pl.program_id(0)
    pos0 = (i * BLOCK) % seq
    kb_ref[pl.ds(0, BLOCK), :] = kp_ref[...]
    kb_ref[pl.ds(BLOCK, BLOCK), :] = kc_ref[...]
    kb_ref[pl.ds(2 * BLOCK, BLOCK), :] = kn_ref[...]
    vb_ref[pl.ds(0, BLOCK), :] = vp_ref[...]
    vb_ref[pl.ds(BLOCK, BLOCK), :] = vc_ref[...]
    vb_ref[pl.ds(2 * BLOCK, BLOCK), :] = vn_ref[...]

    col = lax.broadcasted_iota(jnp.int32, (1, 3 * BLOCK), 1)
    lo = jnp.where(pos0 == 0, BLOCK, 0)
    hi = jnp.where(pos0 + BLOCK == seq, 2 * BLOCK, 3 * BLOCK)
    pen = jnp.where((col >= lo) & (col < hi), 0.0, -jnp.inf)

    lane = lax.broadcasted_iota(jnp.int32, (3 * BLOCK, PAIR_LANES), 1)
    low_half = lane < HEAD_DIM
    zero = jnp.zeros((3 * BLOCK, PAIR_LANES), BF16)
    pairs_per_kv = HEADS_PER_KV // 2

    def scores(kv):
        kk = kb_ref[:, pl.ds(kv * PAIR_LANES, PAIR_LANES)]
        k_half = (jnp.where(low_half, kk, zero), jnp.where(low_half, zero, kk))
        for pr in range(pairs_per_kv):
            pair = kv * pairs_per_kv + pr
            qp = q_ref[:, pl.ds(pair * PAIR_LANES, PAIR_LANES)]
            for par in range(2):
                s = lax.dot_general(qp, k_half[par], (((1,), (1,)), ((), ())), preferred_element_type=F32)
                s_ref[2 * pair + par] = s * (HEAD_DIM ** -0.5 * LOG2E) + bias_ref[2 * pair + par] + pen

    def softmax(kv):
        for head in range(kv * HEADS_PER_KV, (kv + 1) * HEADS_PER_KV):
            sk = sink_ref[head] * LOG2E
            m = jnp.maximum(jnp.max(s_ref[head], axis=-1, keepdims=True), sk)
            e = jnp.exp2(s_ref[head] - m)
            denom = jnp.sum(e, axis=-1, keepdims=True) + jnp.exp2(sk - m)
            e_ref[head] = e.astype(BF16)
            rd_ref[head] = 1.0 / denom

    def values(kv):
        vv = vb_ref[:, pl.ds(kv * PAIR_LANES, PAIR_LANES)]
        v_half = (jnp.where(low_half, vv, zero), jnp.where(low_half, zero, vv))
        for pr in range(pairs_per_kv):
            pair = kv * pairs_per_kv + pr
            o_pair = (_dot(e_ref[2 * pair], v_half[0]) * rd_ref[2 * pair]
                      + _dot(e_ref[2 * pair + 1], v_half[1]) * rd_ref[2 * pair + 1])
            ob_ref[:, pl.ds(pair * PAIR_LANES, PAIR_LANES)] = o_pair.astype(BF16)

    for step in range(N_KV_HEADS + 2):
        if step < N_KV_HEADS:
            scores(step)
        if 0 <= step - 1 < N_KV_HEADS:
            softmax(step - 1)
        if 0 <= step - 2 < N_KV_HEADS:
            values(step - 2)
    o_ref[...] = x_ref[...] + _dot(ob_ref[...], wo_ref[...])


def _attn(x, qkv, seq, bias, wo, sink):
    n = x.shape[0]
    nblk = n // BLOCK
    kcol = D_MODEL // KV_DUP_WIDTH
    vcol = kcol + 1
    prev = lambda i: jnp.maximum(i - 1, 0)
    nxt = lambda i: jnp.minimum(i + 1, nblk - 1)
    kern = functools.partial(_attn_kernel, seq=seq)
    return pl.pallas_call(
        kern,
        grid=(nblk,),
        in_specs=[
            pl.BlockSpec(memory_space=pltpu.SMEM),
            pl.BlockSpec((BLOCK, D_MODEL), lambda i: (i, 0)),
            pl.BlockSpec((BLOCK, KV_DUP_WIDTH), lambda i: (prev(i), kcol)),
            pl.BlockSpec((BLOCK, KV_DUP_WIDTH), lambda i: (i, kcol)),
            pl.BlockSpec((BLOCK, KV_DUP_WIDTH), lambda i: (nxt(i), kcol)),
            pl.BlockSpec((BLOCK, KV_DUP_WIDTH), lambda i: (prev(i), vcol)),
            pl.BlockSpec((BLOCK, KV_DUP_WIDTH), lambda i: (i, vcol)),
            pl.BlockSpec((BLOCK, KV_DUP_WIDTH), lambda i: (nxt(i), vcol)),
            pl.BlockSpec((N_HEADS, BLOCK, 3 * BLOCK), lambda i: (0, 0, 0)),
            pl.BlockSpec((D_MODEL, D_MODEL), lambda i: (0, 0)),
            pl.BlockSpec((BLOCK, D_MODEL), lambda i: (i, 0)),
        ],
        out_specs=pl.BlockSpec((BLOCK, D_MODEL), lambda i: (i, 0)),
        out_shape=jax.ShapeDtypeStruct((n, D_MODEL), F32),
        scratch_shapes=[
            pltpu.VMEM((3 * BLOCK, KV_DUP_WIDTH), BF16),
            pltpu.VMEM((3 * BLOCK, KV_DUP_WIDTH), BF16),
            pltpu.VMEM((BLOCK, D_MODEL), BF16),
            pltpu.VMEM((N_HEADS, BLOCK, 3 * BLOCK), F32),
            pltpu.VMEM((N_HEADS, BLOCK, 3 * BLOCK), BF16),
            pltpu.VMEM((N_HEADS, BLOCK, 1), F32),
        ],
        compiler_params=pltpu.CompilerParams(
            dimension_semantics=("arbitrary",), vmem_limit_bytes=VMEM_LIMIT_BYTES),
        name="window_attn",
    )(sink, qkv, qkv, qkv, qkv, qkv, qkv, qkv, bias, wo, x)


def _alibi_bias():
    h = np.arange(1, N_HEADS + 1, dtype=np.float32)
    slopes = jnp.asarray(2.0 ** (-8.0 * h / N_HEADS), dtype=F32)
    qi = jnp.arange(BLOCK)[:, None]
    kc = jnp.arange(3 * BLOCK)[None, :]
    dist = jnp.abs(kc - BLOCK - qi)
    bias = -slopes[:, None, None] * dist.astype(F32)[None] * LOG2E
    return jnp.where((dist <= WINDOW)[None], bias, -jnp.inf)


def _extend_qkv_weight(w_qkv):
    kv_width = N_KV_HEADS * HEAD_DIM
    wq = w_qkv[:, :D_MODEL]
    wk = w_qkv[:, D_MODEL:D_MODEL + kv_width]
    wv = w_qkv[:, D_MODEL + kv_width:]

    def dup(w):
        w = w.reshape(D_MODEL, N_KV_HEADS, 1, HEAD_DIM)
        return jnp.broadcast_to(w, (D_MODEL, N_KV_HEADS, 2, HEAD_DIM)).reshape(D_MODEL, KV_DUP_WIDTH)

    return jnp.concatenate([wq, dup(wk), dup(wv)], axis=1).astype(BF16)


def _trunk(x3, params):
    b, s, d = x3.shape
    x = x3.reshape(b * s, d)
    for i in range(DEPTH):
        jm = i // 2
        g_mix = params["norm_mix"][i][None]
        if i % 2 == 0:
            x = _pool(x, s, g_mix, params["pool_w"][jm], params["pool_scale"][jm][None])
        else:
            qkv = _qkv(x, g_mix, params["wqkv_ext"][jm])
            x = _attn(x, qkv, s, params["bias"], params["attn_wo"][jm], params["attn_sink"][jm])
        x = _ffn(x, s, params["norm_ffn"][i][None], params["ffn_wup"][i], params["ffn_conv_w"][i],
                 params["ffn_conv_b"][i][None], params["ffn_wdown"][i], params["norm_final"][None],
                 final_norm=(i == DEPTH - 1))
    return x.reshape(b, s, d)


def _prepare_params(norm_mix, norm_ffn, norm_final, pool_w, pool_scale, attn_wqkv, attn_wo, attn_sink,
                    ffn_wup, ffn_conv_w, ffn_conv_b, ffn_wdown):
    params = {
        "norm_mix": norm_mix,
        "norm_ffn": norm_ffn,
        "norm_final": norm_final,
        "pool_w": pool_w.astype(BF16),
        "pool_scale": pool_scale,
        "wqkv_ext": [_extend_qkv_weight(attn_wqkv[j]) for j in range(attn_wqkv.shape[0])],
        "attn_wo": attn_wo.astype(BF16),
        "attn_sink": attn_sink,
        "bias": _alibi_bias(),
        "ffn_wup": _interleave_gate_value(ffn_wup).astype(BF16),
        "ffn_conv_w": _interleave_gate_value(ffn_conv_w),
        "ffn_conv_b": _interleave_gate_value(ffn_conv_b),
        "ffn_wdown": ffn_wdown.astype(BF16),
    }
    return params


def kernel(x_prompt, x_sample, norm_mix, norm_ffn, norm_final, pool_w, pool_scale, attn_wqkv, attn_wo,
           attn_sink, ffn_wup, ffn_conv_w, ffn_conv_b, ffn_wdown):
    params = _prepare_params(norm_mix, norm_ffn, norm_final, pool_w, pool_scale, attn_wqkv, attn_wo,
                             attn_sink, ffn_wup, ffn_conv_w, ffn_conv_b, ffn_wdown)
    return (_trunk(x_prompt, params), _trunk(x_sample, params))
```

```python
import functools

import numpy as np
import jax
import jax.numpy as jnp
from jax import lax
from jax.experimental import pallas as pl
from jax.experimental.pallas import tpu as pltpu

D_MODEL = 2048
DEPTH = 4
POOL_WINDOWS = (2, 4, 8, 16)
POOL_GROUP = D_MODEL // len(POOL_WINDOWS)
HEAD_DIM = 64
N_HEADS = D_MODEL // HEAD_DIM
N_KV_HEADS = N_HEADS // 8
HEADS_PER_KV = N_HEADS // N_KV_HEADS
WINDOW = 128
BLOCK = 128
D_FF = 5632
EPS = 1e-6
LOG2E = 1.4426950408889634

F32 = jnp.float32
BF16 = jnp.bfloat16

SUBLANES = 8
LANES = 128
BF16_ROWS = 16

PAIR_LANES = 2 * HEAD_DIM
KV_DUP_WIDTH = N_KV_HEADS * PAIR_LANES
QKV_EXT_WIDTH = D_MODEL + 2 * KV_DUP_WIDTH

VMEM_LIMIT_BYTES = 56 * 1024 * 1024

FFN_TM = 512
FFN_TF = 512
FFN_EDGE = BF16_ROWS
POOL_TM = 512
POOL_HALO = SUBLANES
QKV_TM = 1024
QKV_TN = 1024


def _rms(x, g):
    ms = jnp.mean(x * x, axis=-1, keepdims=True)
    return x * lax.rsqrt(ms + EPS) * g


def _dot(a, b):
    return jnp.dot(a, b, preferred_element_type=F32)


def _ffn_kernel(x_ref, xp_ref, xn_ref, g_ref, wug_ref, wuv_ref, cwg_ref, cwv_ref, cbg_ref, cbv_ref, wd_ref, gf_ref,
                o_ref, hb_ref, acc_ref, *, seq, tm, tf, final_norm):
    i = pl.program_id(0)
    j = pl.program_id(1)
    nv = tm // SUBLANES

    @pl.when(j == 0)
    def _():
        g = g_ref[...]
        pos0 = (i * tm) % seq
        xperm = jnp.swapaxes(x_ref[...].reshape(SUBLANES, nv, D_MODEL), 0, 1).reshape(tm, D_MODEL)
        hb_ref[pl.ds(0, tm), :] = _rms(xperm, g).astype(BF16)
        acc_ref[...] = xperm
        hp = jnp.where(pos0 == 0, 0.0, _rms(xp_ref[...], g))[POOL_HALO - 1:POOL_HALO]
        hn = jnp.where(pos0 + tm == seq, 0.0, _rms(xn_ref[...], g))[0:1]
        edge = jnp.concatenate([hn, hp, jnp.zeros((FFN_EDGE - 2, D_MODEL), F32)], axis=0)
        hb_ref[pl.ds(tm, FFN_EDGE), :] = edge.astype(BF16)

    hb = hb_ref[...]
    sub = lax.broadcasted_iota(jnp.int32, (SUBLANES, tf), 0)

    def conv(u, cw_ref, cb_ref):
        first_prev = jnp.where(sub == 0, u[tm + 1:tm + 2], pltpu.roll(u[tm - SUBLANES:tm], 1, 0))
        last_next = jnp.where(sub == SUBLANES - 1, u[tm:tm + 1], pltpu.roll(u[0:SUBLANES], SUBLANES - 1, 0))
        u_prev = jnp.concatenate([first_prev, u[:tm - SUBLANES]], axis=0)
        u_next = jnp.concatenate([u[SUBLANES:tm], last_next], axis=0)
        cw = cw_ref[...]
        return u_prev * cw[0:1] + u[:tm] * cw[1:2] + u_next * cw[2:3] + cb_ref[...]

    gate = conv(_dot(hb, wug_ref[...]), cwg_ref, cbg_ref)
    val = conv(_dot(hb, wuv_ref[...]), cwv_ref, cbv_ref)
    act = gate / (1.0 + jnp.exp(-gate)) * val
    acc_ref[...] += _dot(act.astype(BF16), wd_ref[...])

    @pl.when(j == pl.num_programs(1) - 1)
    def _():
        out = jnp.swapaxes(acc_ref[...].reshape(nv, SUBLANES, D_MODEL), 0, 1).reshape(tm, D_MODEL)
        if final_norm:
            out = _rms(out, gf_ref[...])
        o_ref[...] = out


def _ffn(x, seq, layer, g, wup, cw, cb, wdown, g_final, final_norm):
    n = x.shape[0]
    tm, tf, halo = FFN_TM, FFN_TF, POOL_HALO
    nf = D_FF // tf
    hpt = tm // halo
    last_halo_block = n // halo - 1
    kern = functools.partial(_ffn_kernel, seq=seq, tm=tm, tf=tf, final_norm=final_norm)
    gate_cols = lambda i, j: (layer, 0, j)
    value_cols = lambda i, j: (layer, 0, nf + j)
    return pl.pallas_call(
        kern,
        grid=(n // tm, nf),
        in_specs=[
            pl.BlockSpec((tm, D_MODEL), lambda i, j: (i, 0)),
            pl.BlockSpec((halo, D_MODEL), lambda i, j: (jnp.maximum(i * hpt - 1, 0), 0)),
            pl.BlockSpec((halo, D_MODEL), lambda i, j: (jnp.minimum((i + 1) * hpt, last_halo_block), 0)),
            pl.BlockSpec((None, 1, D_MODEL), lambda i, j: (layer, 0, 0)),
            pl.BlockSpec((None, D_MODEL, tf), gate_cols),
            pl.BlockSpec((None, D_MODEL, tf), value_cols),
            pl.BlockSpec((None, 3, tf), gate_cols),
            pl.BlockSpec((None, 3, tf), value_cols),
            pl.BlockSpec((None, 1, tf), gate_cols),
            pl.BlockSpec((None, 1, tf), value_cols),
            pl.BlockSpec((None, tf, D_MODEL), lambda i, j: (layer, j, 0)),
            pl.BlockSpec((1, D_MODEL), lambda i, j: (0, 0)),
        ],
        out_specs=pl.BlockSpec((tm, D_MODEL), lambda i, j: (i, 0)),
        out_shape=jax.ShapeDtypeStruct((n, D_MODEL), F32),
        scratch_shapes=[
            pltpu.VMEM((tm + FFN_EDGE, D_MODEL), BF16),
            pltpu.VMEM((tm, D_MODEL), F32),
        ],
        compiler_params=pltpu.CompilerParams(
            dimension_semantics=("arbitrary", "arbitrary"), vmem_limit_bytes=VMEM_LIMIT_BYTES),
        name="conv_ffn",
    )(x, x, x, g, wup, wup, cw, cw, cb, cb, wdown, g_final)


def _pool_kernel(x_ref, xp_ref, xn_ref, g_ref, w_ref, sc_ref, o_ref, hh_ref, *, seq, tm):
    i = pl.program_id(0)
    halo = POOL_HALO
    pos0 = (i * tm) % seq
    g = g_ref[...]
    x = x_ref[...]
    h = _rms(x, g)
    hh_ref[pl.ds(halo, tm), :] = h
    hh_ref[pl.ds(0, halo), :] = jnp.where(pos0 == 0, 0.0, _rms(xp_ref[...], g))
    hh_ref[pl.ds(halo + tm, halo), :] = jnp.where(pos0 + tm == seq, 0.0, _rms(xn_ref[...], g))
    t = pos0 + lax.broadcasted_iota(jnp.int32, (tm, 1), 0)
    for gi, win in enumerate(POOL_WINDOWS):
        half = win // 2
        lanes = pl.ds(gi * POOL_GROUP, POOL_GROUP)
        acc = hh_ref[pl.ds(halo - half, tm + win), lanes]
        k = 1
        while k < win:
            length = acc.shape[0] - k
            acc = acc[:length] + acc[k:]
            k *= 2
        acc = acc[:tm]
        cnt = (jnp.clip(t + half, 0, seq) - jnp.clip(t - half, 0, seq)).astype(F32)
        p = acc / cnt - hh_ref[pl.ds(halo, tm), lanes]
        y = _dot(p.astype(BF16), w_ref[gi]) * sc_ref[:, lanes]
        o_ref[:, lanes] = x_ref[:, lanes] + y


def _pool(x, seq, layer, mixer, g, w, scale):
    n = x.shape[0]
    tm, halo = POOL_TM, POOL_HALO
    hpt = tm // halo
    last_halo_block = n // halo - 1
    kern = functools.partial(_pool_kernel, seq=seq, tm=tm)
    return pl.pallas_call(
        kern,
        grid=(n // tm,),
        in_specs=[
            pl.BlockSpec((tm, D_MODEL), lambda i: (i, 0)),
            pl.BlockSpec((halo, D_MODEL), lambda i: (jnp.maximum(i * hpt - 1, 0), 0)),
            pl.BlockSpec((halo, D_MODEL), lambda i: (jnp.minimum((i + 1) * hpt, last_halo_block), 0)),
            pl.BlockSpec((None, 1, D_MODEL), lambda i: (layer, 0, 0)),
            pl.BlockSpec((None, len(POOL_WINDOWS), POOL_GROUP, POOL_GROUP), lambda i: (mixer, 0, 0, 0)),
            pl.BlockSpec((None, 1, D_MODEL), lambda i: (mixer, 0, 0)),
        ],
        out_specs=pl.BlockSpec((tm, D_MODEL), lambda i: (i, 0)),
        out_shape=jax.ShapeDtypeStruct((n, D_MODEL), F32),
        scratch_shapes=[pltpu.VMEM((tm + 2 * halo, D_MODEL), F32)],
        compiler_params=pltpu.CompilerParams(
            dimension_semantics=("arbitrary",), vmem_limit_bytes=VMEM_LIMIT_BYTES),
        name="pool_mixer",
    )(x, x, x, g, w, scale)


def _qkv_kernel(x_ref, g_ref, w_ref, o_ref, hb_ref):
    @pl.when(pl.program_id(1) == 0)
    def _():
        hb_ref[...] = _rms(x_ref[...], g_ref[...]).astype(BF16)

    o_ref[...] = _dot(hb_ref[...], w_ref[...]).astype(BF16)


def _qkv(x, layer, mixer, g, w_ext):
    n = x.shape[0]
    tm, tn = QKV_TM, QKV_TN
    return pl.pallas_call(
        _qkv_kernel,
        grid=(n // tm, QKV_EXT_WIDTH // tn),
        in_specs=[
            pl.BlockSpec((tm, D_MODEL), lambda i, j: (i, 0)),
            pl.BlockSpec((None, 1, D_MODEL), lambda i, j: (layer, 0, 0)),
            pl.BlockSpec((None, D_MODEL, tn), lambda i, j: (mixer, 0, j)),
        ],
        out_specs=pl.BlockSpec((tm, tn), lambda i, j: (i, j)),
        out_shape=jax.ShapeDtypeStruct((n, QKV_EXT_WIDTH), BF16),
        scratch_shapes=[pltpu.VMEM((tm, D_MODEL), BF16)],
        compiler_params=pltpu.CompilerParams(
            dimension_semantics=("arbitrary", "arbitrary"), vmem_limit_bytes=VMEM_LIMIT_BYTES),
        name="qkv_proj",
    )(x, g, w_ext)


def _attn_kernel(sink_ref, q_ref, kp_ref, kc_ref, kn_ref, vp_ref, vc_ref, vn_ref, bias_ref, wo_ref, x_ref,
                 o_ref, kb_ref, vb_ref, ob_ref, s_ref, e_ref, rd_ref, *, seq, mixer):
    i = pl.program_id(0)
    pos0 = (i * BLOCK) % seq
    kb_ref[pl.ds(0, BLOCK), :] = kp_ref[...]
    kb_ref[pl.ds(BLOCK, BLOCK), :] = kc_ref[...]
    kb_ref[pl.ds(2 * BLOCK, BLOCK), :] = kn_ref[...]
    vb_ref[pl.ds(0, BLOCK), :] = vp_ref[...]
    vb_ref[pl.ds(BLOCK, BLOCK), :] = vc_ref[...]
    vb_ref[pl.ds(2 * BLOCK, BLOCK), :] = vn_ref[...]

    kind = jnp.where(pos0 == 0, 1, jnp.where(pos0 + BLOCK == seq, 2, 0))

    lane = lax.broadcasted_iota(jnp.int32, (3 * BLOCK, PAIR_LANES), 1)
    low_half = lane < HEAD_DIM
    zero = jnp.zeros((3 * BLOCK, PAIR_LANES), BF16)
    pairs_per_kv = HEADS_PER_KV // 2

    def scores(kv):
        kk = kb_ref[:, pl.ds(kv * PAIR_LANES, PAIR_LANES)]
        k_half = (jnp.where(low_half, kk, zero), jnp.where(low_half, zero, kk))
        for pr in range(pairs_per_kv):
            pair = kv * pairs_per_kv + pr
            qp = q_ref[:, pl.ds(pair * PAIR_LANES, PAIR_LANES)]
            for par in range(2):
                s = lax.dot_general(qp, k_half[par], (((1,), (1,)), ((), ())), preferred_element_type=F32)
                s_ref[2 * pair + par] = s * (HEAD_DIM ** -0.5 * LOG2E) + bias_ref[kind, 2 * pair + par]

    def softmax(kv):
        for head in range(kv * HEADS_PER_KV, (kv + 1) * HEADS_PER_KV):
            sk = sink_ref[mixer, head] * LOG2E
            m = jnp.maximum(jnp.max(s_ref[head], axis=-1, keepdims=True), sk)
            e = jnp.exp2(s_ref[head] - m)
            denom = jnp.sum(e, axis=-1, keepdims=True) + jnp.exp2(sk - m)
            e_ref[head] = e.astype(BF16)
            rd_ref[head] = 1.0 / denom

    def values(kv):
        vv = vb_ref[:, pl.ds(kv * PAIR_LANES, PAIR_LANES)]
        v_half = (jnp.where(low_half, vv, zero), jnp.where(low_half, zero, vv))
        for pr in range(pairs_per_kv):
            pair = kv * pairs_per_kv + pr
            o_pair = (_dot(e_ref[2 * pair], v_half[0]) * rd_ref[2 * pair]
                      + _dot(e_ref[2 * pair + 1], v_half[1]) * rd_ref[2 * pair + 1])
            ob_ref[:, pl.ds(pair * PAIR_LANES, PAIR_LANES)] = o_pair.astype(BF16)

    for step in range(N_KV_HEADS + 2):
        if step < N_KV_HEADS:
            scores(step)
        if 0 <= step - 1 < N_KV_HEADS:
            softmax(step - 1)
        if 0 <= step - 2 < N_KV_HEADS:
            values(step - 2)
    o_ref[...] = x_ref[...] + _dot(ob_ref[...], wo_ref[...])


def _attn(x, qkv, seq, mixer, bias, wo, sink):
    n = x.shape[0]
    assert seq % BLOCK == 0 and seq > BLOCK, "a block is the first or the last of its sequence, never both"
    nblk = n // BLOCK
    kcol = D_MODEL // KV_DUP_WIDTH
    vcol = kcol + 1
    prev = lambda i: jnp.maximum(i - 1, 0)
    nxt = lambda i: jnp.minimum(i + 1, nblk - 1)
    kern = functools.partial(_attn_kernel, seq=seq, mixer=mixer)
    return pl.pallas_call(
        kern,
        grid=(nblk,),
        in_specs=[
            pl.BlockSpec(memory_space=pltpu.SMEM),
            pl.BlockSpec((BLOCK, D_MODEL), lambda i: (i, 0)),
            pl.BlockSpec((BLOCK, KV_DUP_WIDTH), lambda i: (prev(i), kcol)),
            pl.BlockSpec((BLOCK, KV_DUP_WIDTH), lambda i: (i, kcol)),
            pl.BlockSpec((BLOCK, KV_DUP_WIDTH), lambda i: (nxt(i), kcol)),
            pl.BlockSpec((BLOCK, KV_DUP_WIDTH), lambda i: (prev(i), vcol)),
            pl.BlockSpec((BLOCK, KV_DUP_WIDTH), lambda i: (i, vcol)),
            pl.BlockSpec((BLOCK, KV_DUP_WIDTH), lambda i: (nxt(i), vcol)),
            pl.BlockSpec((3, N_HEADS, BLOCK, 3 * BLOCK), lambda i: (0, 0, 0, 0)),
            pl.BlockSpec((None, D_MODEL, D_MODEL), lambda i: (mixer, 0, 0)),
            pl.BlockSpec((BLOCK, D_MODEL), lambda i: (i, 0)),
        ],
        out_specs=pl.BlockSpec((BLOCK, D_MODEL), lambda i: (i, 0)),
        out_shape=jax.ShapeDtypeStruct((n, D_MODEL), F32),
        scratch_shapes=[
            pltpu.VMEM((3 * BLOCK, KV_DUP_WIDTH), BF16),
            pltpu.VMEM((3 * BLOCK, KV_DUP_WIDTH), BF16),
            pltpu.VMEM((BLOCK, D_MODEL), BF16),
            pltpu.VMEM((N_HEADS, BLOCK, 3 * BLOCK), F32),
            pltpu.VMEM((N_HEADS, BLOCK, 3 * BLOCK), BF16),
            pltpu.VMEM((N_HEADS, BLOCK, 1), F32),
        ],
        compiler_params=pltpu.CompilerParams(
            dimension_semantics=("arbitrary",), vmem_limit_bytes=VMEM_LIMIT_BYTES),
        name="window_attn",
    )(sink, qkv, qkv, qkv, qkv, qkv, qkv, qkv, bias, wo, x)


def _alibi_bias():
    h = np.arange(1, N_HEADS + 1, dtype=np.float32)
    slopes = jnp.asarray(2.0 ** (-8.0 * h / N_HEADS), dtype=F32)
    qi = jnp.arange(BLOCK)[:, None]
    kc = jnp.arange(3 * BLOCK)[None, :]
    dist = jnp.abs(kc - BLOCK - qi)
    bias = -slopes[:, None, None] * dist.astype(F32)[None] * LOG2E
    in_window = dist <= WINDOW
    valid = jnp.stack([in_window, in_window & (kc >= BLOCK), in_window & (kc < 2 * BLOCK)])
    return jnp.where(valid[:, None], bias[None], -jnp.inf)


def _extend_qkv_weight(w_qkv):
    lead = w_qkv.shape[:-1]
    kv_width = N_KV_HEADS * HEAD_DIM
    wq = w_qkv[..., :D_MODEL]
    wk = w_qkv[..., D_MODEL:D_MODEL + kv_width]
    wv = w_qkv[..., D_MODEL + kv_width:]

    def dup(w):
        w = w.reshape(*lead, N_KV_HEADS, 1, HEAD_DIM)
        return jnp.broadcast_to(w, (*lead, N_KV_HEADS, 2, HEAD_DIM)).reshape(*lead, KV_DUP_WIDTH)

    return jnp.concatenate([wq.astype(BF16), dup(wk.astype(BF16)), dup(wv.astype(BF16))], axis=-1)


def _trunk(x3, params):
    b, s, d = x3.shape
    x = x3.reshape(b * s, d)
    for i in range(DEPTH):
        jm = i // 2
        if i % 2 == 0:
            x = _pool(x, s, i, jm, params["norm_mix"], params["pool_w"], params["pool_scale"])
        else:
            qkv = _qkv(x, i, jm, params["norm_mix"], params["wqkv_ext"])
            x = _attn(x, qkv, s, jm, params["bias"], params["attn_wo"], params["attn_sink"])
        x = _ffn(x, s, i, params["norm_ffn"], params["ffn_wup"], params["ffn_conv_w"], params["ffn_conv_b"],
                 params["ffn_wdown"], params["norm_final"], final_norm=(i == DEPTH - 1))
    return x.reshape(b, s, d)


def _prepare_params(norm_mix, norm_ffn, norm_final, pool_w, pool_scale, attn_wqkv, attn_wo, attn_sink,
                    ffn_wup, ffn_conv_w, ffn_conv_b, ffn_wdown):
    return {
        "norm_mix": norm_mix[:, None, :],
        "norm_ffn": norm_ffn[:, None, :],
        "norm_final": norm_final[None, :],
        "pool_w": pool_w.astype(BF16),
        "pool_scale": pool_scale[:, None, :],
        "wqkv_ext": _extend_qkv_weight(attn_wqkv),
        "attn_wo": attn_wo.astype(BF16),
        "attn_sink": attn_sink,
        "bias": _alibi_bias(),
        "ffn_wup": ffn_wup.astype(BF16),
        "ffn_conv_w": ffn_conv_w,
        "ffn_conv_b": ffn_conv_b[:, None, :],
        "ffn_wdown": ffn_wdown.astype(BF16),
    }


def kernel(x_prompt, x_sample, norm_mix, norm_ffn, norm_final, pool_w, pool_scale, attn_wqkv, attn_wo,
           attn_sink, ffn_wup, ffn_conv_w, ffn_conv_b, ffn_wdown):
    params = _prepare_params(norm_mix, norm_ffn, norm_final, pool_w, pool_scale, attn_wqkv, attn_wo,
                             attn_sink, ffn_wup, ffn_conv_w, ffn_conv_b, ffn_wdown)
    return (_trunk(x_prompt, params), _trunk(x_sample, params))
```

```python
import functools

import numpy as np
import jax
import jax.numpy as jnp
from jax import lax
from jax.experimental import pallas as pl
from jax.experimental.pallas import tpu as pltpu

D_MODEL = 2048
DEPTH = 4
POOL_WINDOWS = (2, 4, 8, 16)
POOL_GROUP = D_MODEL // len(POOL_WINDOWS)
HEAD_DIM = 64
N_HEADS = D_MODEL // HEAD_DIM
N_KV_HEADS = N_HEADS // 8
HEADS_PER_KV = N_HEADS // N_KV_HEADS
WINDOW = 128
BLOCK = 128
D_FF = 5632
EPS = 1e-6
LOG2E = 1.4426950408889634

F32 = jnp.float32
BF16 = jnp.bfloat16

SUBLANES = 8
LANES = 128
BF16_ROWS = 16

PAIR_LANES = 2 * HEAD_DIM
KV_DUP_WIDTH = N_KV_HEADS * PAIR_LANES
QKV_EXT_WIDTH = D_MODEL + 2 * KV_DUP_WIDTH

VMEM_LIMIT_BYTES = 56 * 1024 * 1024

FFN_TM = 512
FFN_TF = 512
FFN_ROW_BLOCKS = 4
FFN_EDGE = BF16_ROWS
POOL_TM = 512
POOL_HALO = SUBLANES
QKV_TM = 1024
QKV_TN = 1024


def _rms(x, g):
    ms = jnp.mean(x * x, axis=-1, keepdims=True)
    return x * lax.rsqrt(ms + EPS) * g


def _dot(a, b):
    return jnp.dot(a, b, preferred_element_type=F32)


def _ffn_kernel(x_ref, xp_ref, xn_ref, g_ref, wug_ref, wuv_ref, cwg_ref, cwv_ref, cbg_ref, cbv_ref, wd_ref, gf_ref,
                o_ref, hb_ref, acc_ref, *, seq, tm, tf, final_norm):
    i = pl.program_id(0)
    j = pl.program_id(1)
    nv = tm // SUBLANES

    @pl.when(j == 0)
    def _():
        g = g_ref[...]
        pos0 = (i * tm) % seq
        xperm = jnp.swapaxes(x_ref[...].reshape(SUBLANES, nv, D_MODEL), 0, 1).reshape(tm, D_MODEL)
        hb_ref[pl.ds(0, tm), :] = _rms(xperm, g).astype(BF16)
        acc_ref[...] = xperm
        hp = jnp.where(pos0 == 0, 0.0, _rms(xp_ref[...], g))[POOL_HALO - 1:POOL_HALO]
        hn = jnp.where(pos0 + tm == seq, 0.0, _rms(xn_ref[...], g))[0:1]
        edge = jnp.concatenate([hn, hp, jnp.zeros((FFN_EDGE - 2, D_MODEL), F32)], axis=0)
        hb_ref[pl.ds(tm, FFN_EDGE), :] = edge.astype(BF16)

    sub = lax.broadcasted_iota(jnp.int32, (SUBLANES, tf), 0)
    rb = tm // FFN_ROW_BLOCKS
    last = FFN_ROW_BLOCKS - 1

    def up(w_ref):
        parts = [None] * FFN_ROW_BLOCKS
        for r in (0, last) + tuple(range(1, last)):
            size = rb + FFN_EDGE if r == last else rb
            parts[r] = _dot(hb_ref[pl.ds(r * rb, size), :], w_ref[...])
        return jnp.concatenate(parts, axis=0)

    def neighbours(u):
        first_prev = jnp.where(sub == 0, u[tm + 1:tm + 2], pltpu.roll(u[tm - SUBLANES:tm], 1, 0))
        last_next = jnp.where(sub == SUBLANES - 1, u[tm:tm + 1], pltpu.roll(u[0:SUBLANES], SUBLANES - 1, 0))
        u_prev = jnp.concatenate([first_prev, u[:tm - SUBLANES]], axis=0)
        u_next = jnp.concatenate([u[SUBLANES:tm], last_next], axis=0)
        return u_prev, u, u_next

    def conv(taps, rows, cw_ref, cb_ref):
        cw = cw_ref[...]
        return taps[0][rows] * cw[0:1] + taps[1][rows] * cw[1:2] + taps[2][rows] * cw[2:3] + cb_ref[...]

    gate_taps = neighbours(up(wug_ref))
    val_taps = neighbours(up(wuv_ref))
    for r in (0,) + tuple(range(1, last)) + (last,):
        rows = slice(r * rb, (r + 1) * rb)
        gate = conv(gate_taps, rows, cwg_ref, cbg_ref)
        val = conv(val_taps, rows, cwv_ref, cbv_ref)
        act = gate / (1.0 + jnp.exp(-gate)) * val
        acc_ref[pl.ds(r * rb, rb), :] += _dot(act.astype(BF16), wd_ref[...])

    @pl.when(j == pl.num_programs(1) - 1)
    def _():
        out = jnp.swapaxes(acc_ref[...].reshape(nv, SUBLANES, D_MODEL), 0, 1).reshape(tm, D_MODEL)
        if final_norm:
            out = _rms(out, gf_ref[...])
        o_ref[...] = out


def _ffn(x, seq, layer, g, wup, cw, cb, wdown, g_final, final_norm):
    n = x.shape[0]
    tm, tf, halo = FFN_TM, FFN_TF, POOL_HALO
    nf = D_FF // tf
    hpt = tm // halo
    last_halo_block = n // halo - 1
    kern = functools.partial(_ffn_kernel, seq=seq, tm=tm, tf=tf, final_norm=final_norm)
    gate_cols = lambda i, j: (layer, 0, j)
    value_cols = lambda i, j: (layer, 0, nf + j)
    return pl.pallas_call(
        kern,
        grid=(n // tm, nf),
        in_specs=[
            pl.BlockSpec((tm, D_MODEL), lambda i, j: (i, 0)),
            pl.BlockSpec((halo, D_MODEL), lambda i, j: (jnp.maximum(i * hpt - 1, 0), 0)),
            pl.BlockSpec((halo, D_MODEL), lambda i, j: (jnp.minimum((i + 1) * hpt, last_halo_block), 0)),
            pl.BlockSpec((None, 1, D_MODEL), lambda i, j: (layer, 0, 0)),
            pl.BlockSpec((None, D_MODEL, tf), gate_cols),
            pl.BlockSpec((None, D_MODEL, tf), value_cols),
            pl.BlockSpec((None, 3, tf), gate_cols),
            pl.BlockSpec((None, 3, tf), value_cols),
            pl.BlockSpec((None, 1, tf), gate_cols),
            pl.BlockSpec((None, 1, tf), value_cols),
            pl.BlockSpec((None, tf, D_MODEL), lambda i, j: (layer, j, 0)),
            pl.BlockSpec((1, D_MODEL), lambda i, j: (0, 0)),
        ],
        out_specs=pl.BlockSpec((tm, D_MODEL), lambda i, j: (i, 0)),
        out_shape=jax.ShapeDtypeStruct((n, D_MODEL), F32),
        scratch_shapes=[
            pltpu.VMEM((tm + FFN_EDGE, D_MODEL), BF16),
            pltpu.VMEM((tm, D_MODEL), F32),
        ],
        compiler_params=pltpu.CompilerParams(
            dimension_semantics=("arbitrary", "arbitrary"), vmem_limit_bytes=VMEM_LIMIT_BYTES),
        name="conv_ffn",
    )(x, x, x, g, wup, wup, cw, cw, cb, cb, wdown, g_final)


def _pool_kernel(x_ref, xp_ref, xn_ref, g_ref, w_ref, sc_ref, o_ref, hh_ref, *, seq, tm):
    i = pl.program_id(0)
    halo = POOL_HALO
    pos0 = (i * tm) % seq
    g = g_ref[...]
    x = x_ref[...]
    h = _rms(x, g)
    hh_ref[pl.ds(halo, tm), :] = h
    hh_ref[pl.ds(0, halo), :] = jnp.where(pos0 == 0, 0.0, _rms(xp_ref[...], g))
    hh_ref[pl.ds(halo + tm, halo), :] = jnp.where(pos0 + tm == seq, 0.0, _rms(xn_ref[...], g))
    t = pos0 + lax.broadcasted_iota(jnp.int32, (tm, 1), 0)
    for gi, win in enumerate(POOL_WINDOWS):
        half = win // 2
        lanes = pl.ds(gi * POOL_GROUP, POOL_GROUP)
        acc = hh_ref[pl.ds(halo - half, tm + win), lanes]
        k = 1
        while k < win:
            length = acc.shape[0] - k
            acc = acc[:length] + acc[k:]
            k *= 2
        acc = acc[:tm]
        cnt = (jnp.clip(t + half, 0, seq) - jnp.clip(t - half, 0, seq)).astype(F32)
        p = acc / cnt - hh_ref[pl.ds(halo, tm), lanes]
        y = _dot(p.astype(BF16), w_ref[gi]) * sc_ref[:, lanes]
        o_ref[:, lanes] = x_ref[:, lanes] + y


def _pool(x, seq, layer, mixer, g, w, scale):
    n = x.shape[0]
    tm, halo = POOL_TM, POOL_HALO
    hpt = tm // halo
    last_halo_block = n // halo - 1
    kern = functools.partial(_pool_kernel, seq=seq, tm=tm)
    return pl.pallas_call(
        kern,
        grid=(n // tm,),
        in_specs=[
            pl.BlockSpec((tm, D_MODEL), lambda i: (i, 0)),
            pl.BlockSpec((halo, D_MODEL), lambda i: (jnp.maximum(i * hpt - 1, 0), 0)),
            pl.BlockSpec((halo, D_MODEL), lambda i: (jnp.minimum((i + 1) * hpt, last_halo_block), 0)),
            pl.BlockSpec((None, 1, D_MODEL), lambda i: (layer, 0, 0)),
            pl.BlockSpec((None, len(POOL_WINDOWS), POOL_GROUP, POOL_GROUP), lambda i: (mixer, 0, 0, 0)),
            pl.BlockSpec((None, 1, D_MODEL), lambda i: (mixer, 0, 0)),
        ],
        out_specs=pl.BlockSpec((tm, D_MODEL), lambda i: (i, 0)),
        out_shape=jax.ShapeDtypeStruct((n, D_MODEL), F32),
        scratch_shapes=[pltpu.VMEM((tm + 2 * halo, D_MODEL), F32)],
        compiler_params=pltpu.CompilerParams(
            dimension_semantics=("arbitrary",), vmem_limit_bytes=VMEM_LIMIT_BYTES),
        name="pool_mixer",
    )(x, x, x, g, w, scale)


def _qkv_kernel(x_ref, g_ref, w_ref, o_ref, hb_ref):
    @pl.when(pl.program_id(1) == 0)
    def _():
        hb_ref[...] = _rms(x_ref[...], g_ref[...]).astype(BF16)

    o_ref[...] = _dot(hb_ref[...], w_ref[...]).astype(BF16)


def _qkv(x, layer, mixer, g, w_ext):
    n = x.shape[0]
    tm, tn = QKV_TM, QKV_TN
    return pl.pallas_call(
        _qkv_kernel,
        grid=(n // tm, QKV_EXT_WIDTH // tn),
        in_specs=[
            pl.BlockSpec((tm, D_MODEL), lambda i, j: (i, 0)),
            pl.BlockSpec((None, 1, D_MODEL), lambda i, j: (layer, 0, 0)),
            pl.BlockSpec((None, D_MODEL, tn), lambda i, j: (mixer, 0, j)),
        ],
        out_specs=pl.BlockSpec((tm, tn), lambda i, j: (i, j)),
        out_shape=jax.ShapeDtypeStruct((n, QKV_EXT_WIDTH), BF16),
        scratch_shapes=[pltpu.VMEM((tm, D_MODEL), BF16)],
        compiler_params=pltpu.CompilerParams(
            dimension_semantics=("arbitrary", "arbitrary"), vmem_limit_bytes=VMEM_LIMIT_BYTES),
        name="qkv_proj",
    )(x, g, w_ext)


def _attn_kernel(sink_ref, q_ref, kp_ref, kc_ref, kn_ref, vp_ref, vc_ref, vn_ref, bias_ref, wo_ref, x_ref,
                 o_ref, kb_ref, vb_ref, ob_ref, s_ref, e_ref, rd_ref, *, seq, mixer):
    i = pl.program_id(0)
    pos0 = (i * BLOCK) % seq
    kb_ref[pl.ds(0, BLOCK), :] = kp_ref[...]
    kb_ref[pl.ds(BLOCK, BLOCK), :] = kc_ref[...]
    kb_ref[pl.ds(2 * BLOCK, BLOCK), :] = kn_ref[...]
    vb_ref[pl.ds(0, BLOCK), :] = vp_ref[...]
    vb_ref[pl.ds(BLOCK, BLOCK), :] = vc_ref[...]
    vb_ref[pl.ds(2 * BLOCK, BLOCK), :] = vn_ref[...]

    kind = jnp.where(pos0 == 0, 1, jnp.where(pos0 + BLOCK == seq, 2, 0))

    lane = lax.broadcasted_iota(jnp.int32, (3 * BLOCK, PAIR_LANES), 1)
    low_half = lane < HEAD_DIM
    zero = jnp.zeros((3 * BLOCK, PAIR_LANES), BF16)
    pairs_per_kv = HEADS_PER_KV // 2

    def scores(kv):
        kk = kb_ref[:, pl.ds(kv * PAIR_LANES, PAIR_LANES)]
        k_half = (jnp.where(low_half, kk, zero), jnp.where(low_half, zero, kk))
        for pr in range(pairs_per_kv):
            pair = kv * pairs_per_kv + pr
            qp = q_ref[:, pl.ds(pair * PAIR_LANES, PAIR_LANES)]
            for par in range(2):
                s = lax.dot_general(qp, k_half[par], (((1,), (1,)), ((), ())), preferred_element_type=F32)
                s_ref[2 * pair + par] = s * (HEAD_DIM ** -0.5 * LOG2E) + bias_ref[kind, 2 * pair + par]

    def softmax(kv):
        for head in range(kv * HEADS_PER_KV, (kv + 1) * HEADS_PER_KV):
            sk = sink_ref[mixer, head] * LOG2E
            m = jnp.maximum(jnp.max(s_ref[head], axis=-1, keepdims=True), sk)
            e = jnp.exp2(s_ref[head] - m)
            denom = jnp.sum(e, axis=-1, keepdims=True) + jnp.exp2(sk - m)
            e_ref[head] = e.astype(BF16)
            rd_ref[head] = 1.0 / denom

    def values(kv):
        vv = vb_ref[:, pl.ds(kv * PAIR_LANES, PAIR_LANES)]
        v_half = (jnp.where(low_half, vv, zero), jnp.where(low_half, zero, vv))
        for pr in range(pairs_per_kv):
            pair = kv * pairs_per_kv + pr
            o_pair = (_dot(e_ref[2 * pair], v_half[0]) * rd_ref[2 * pair]
                      + _dot(e_ref[2 * pair + 1], v_half[1]) * rd_ref[2 * pair + 1])
            ob_ref[:, pl.ds(pair * PAIR_LANES, PAIR_LANES)] = o_pair.astype(BF16)

    for step in range(N_KV_HEADS + 2):
        if step < N_KV_HEADS:
            scores(step)
        if 0 <= step - 1 < N_KV_HEADS:
            softmax(step - 1)
        if 0 <= step - 2 < N_KV_HEADS:
            values(step - 2)
    o_ref[...] = x_ref[...] + _dot(ob_ref[...], wo_ref[...])


def _attn(x, qkv, seq, mixer, bias, wo, sink):
    n = x.shape[0]
    assert seq % BLOCK == 0 and seq > BLOCK, "a block is the first or the last of its sequence, never both"
    nblk = n // BLOCK
    kcol = D_MODEL // KV_DUP_WIDTH
    vcol = kcol + 1
    prev = lambda i: jnp.maximum(i - 1, 0)
    nxt = lambda i: jnp.minimum(i + 1, nblk - 1)
    kern = functools.partial(_attn_kernel, seq=seq, mixer=mixer)
    return pl.pallas_call(
        kern,
        grid=(nblk,),
        in_specs=[
            pl.BlockSpec(memory_space=pltpu.SMEM),
            pl.BlockSpec((BLOCK, D_MODEL), lambda i: (i, 0)),
            pl.BlockSpec((BLOCK, KV_DUP_WIDTH), lambda i: (prev(i), kcol)),
            pl.BlockSpec((BLOCK, KV_DUP_WIDTH), lambda i: (i, kcol)),
            pl.BlockSpec((BLOCK, KV_DUP_WIDTH), lambda i: (nxt(i), kcol)),
            pl.BlockSpec((BLOCK, KV_DUP_WIDTH), lambda i: (prev(i), vcol)),
            pl.BlockSpec((BLOCK, KV_DUP_WIDTH), lambda i: (i, vcol)),
            pl.BlockSpec((BLOCK, KV_DUP_WIDTH), lambda i: (nxt(i), vcol)),
            pl.BlockSpec((3, N_HEADS, BLOCK, 3 * BLOCK), lambda i: (0, 0, 0, 0)),
            pl.BlockSpec((None, D_MODEL, D_MODEL), lambda i: (mixer, 0, 0)),
            pl.BlockSpec((BLOCK, D_MODEL), lambda i: (i, 0)),
        ],
        out_specs=pl.BlockSpec((BLOCK, D_MODEL), lambda i: (i, 0)),
        out_shape=jax.ShapeDtypeStruct((n, D_MODEL), F32),
        scratch_shapes=[
            pltpu.VMEM((3 * BLOCK, KV_DUP_WIDTH), BF16),
            pltpu.VMEM((3 * BLOCK, KV_DUP_WIDTH), BF16),
            pltpu.VMEM((BLOCK, D_MODEL), BF16),
            pltpu.VMEM((N_HEADS, BLOCK, 3 * BLOCK), F32),
            pltpu.VMEM((N_HEADS, BLOCK, 3 * BLOCK), BF16),
            pltpu.VMEM((N_HEADS, BLOCK, 1), F32),
        ],
        compiler_params=pltpu.CompilerParams(
            dimension_semantics=("arbitrary",), vmem_limit_bytes=VMEM_LIMIT_BYTES),
        name="window_attn",
    )(sink, qkv, qkv, qkv, qkv, qkv, qkv, qkv, bias, wo, x)


def _alibi_bias():
    h = np.arange(1, N_HEADS + 1, dtype=np.float32)
    slopes = jnp.asarray(2.0 ** (-8.0 * h / N_HEADS), dtype=F32)
    qi = jnp.arange(BLOCK)[:, None]
    kc = jnp.arange(3 * BLOCK)[None, :]
    dist = jnp.abs(kc - BLOCK - qi)
    bias = -slopes[:, None, None] * dist.astype(F32)[None] * LOG2E
    in_window = dist <= WINDOW
    valid = jnp.stack([in_window, in_window & (kc >= BLOCK), in_window & (kc < 2 * BLOCK)])
    return jnp.where(valid[:, None], bias[None], -jnp.inf)


def _extend_qkv_weight(w_qkv):
    lead = w_qkv.shape[:-1]
    kv_width = N_KV_HEADS * HEAD_DIM
    wq = w_qkv[..., :D_MODEL]
    wk = w_qkv[..., D_MODEL:D_MODEL + kv_width]
    wv = w_qkv[..., D_MODEL + kv_width:]

    def dup(w):
        w = w.reshape(*lead, N_KV_HEADS, 1, HEAD_DIM)
        return jnp.broadcast_to(w, (*lead, N_KV_HEADS, 2, HEAD_DIM)).reshape(*lead, KV_DUP_WIDTH)

    return jnp.concatenate([wq.astype(BF16), dup(wk.astype(BF16)), dup(wv.astype(BF16))], axis=-1)


def _trunk(x3, params):
    b, s, d = x3.shape
    x = x3.reshape(b * s, d)
    for i in range(DEPTH):
        jm = i // 2
        if i % 2 == 0:
            x = _pool(x, s, i, jm, params["norm_mix"], params["pool_w"], params["pool_scale"])
        else:
            qkv = _qkv(x, i, jm, params["norm_mix"], params["wqkv_ext"])
            x = _attn(x, qkv, s, jm, params["bias"], params["attn_wo"], params["attn_sink"])
        x = _ffn(x, s, i, params["norm_ffn"], params["ffn_wup"], params["ffn_conv_w"], params["ffn_conv_b"],
                 params["ffn_wdown"], params["norm_final"], final_norm=(i == DEPTH - 1))
    return x.reshape(b, s, d)


def _prepare_params(norm_mix, norm_ffn, norm_final, pool_w, pool_scale, attn_wqkv, attn_wo, attn_sink,
                    ffn_wup, ffn_conv_w, ffn_conv_b, ffn_wdown):
    return {
        "norm_mix": norm_mix[:, None, :],
        "norm_ffn": norm_ffn[:, None, :],
        "norm_final": norm_final[None, :],
        "pool_w": pool_w.astype(BF16),
        "pool_scale": pool_scale[:, None, :],
        "wqkv_ext": _extend_qkv_weight(attn_wqkv),
        "attn_wo": attn_wo.astype(BF16),
        "attn_sink": attn_sink,
        "bias": _alibi_bias(),
        "ffn_wup": ffn_wup.astype(BF16),
        "ffn_conv_w": ffn_conv_w,
        "ffn_conv_b": ffn_conv_b[:, None, :],
        "ffn_wdown": ffn_wdown.astype(BF16),
    }


def kernel(x_prompt, x_sample, norm_mix, norm_ffn, norm_final, pool_w, pool_scale, attn_wqkv, attn_wo,
           attn_sink, ffn_wup, ffn_conv_w, ffn_conv_b, ffn_wdown):
    params = _prepare_params(norm_mix, norm_ffn, norm_final, pool_w, pool_scale, attn_wqkv, attn_wo,
                             attn_sink, ffn_wup, ffn_conv_w, ffn_conv_b, ffn_wdown)
    return (_trunk(x_prompt, params), _trunk(x_sample, params))
```

```python
import functools

import numpy as np
import jax
import jax.numpy as jnp
from jax import lax
from jax.experimental import pallas as pl
from jax.experimental.pallas import tpu as pltpu

D_MODEL = 2048
DEPTH = 4
POOL_WINDOWS = (2, 4, 8, 16)
POOL_GROUP = D_MODEL // len(POOL_WINDOWS)
HEAD_DIM = 64
N_HEADS = D_MODEL // HEAD_DIM
N_KV_HEADS = N_HEADS // 8
HEADS_PER_KV = N_HEADS // N_KV_HEADS
WINDOW = 128
BLOCK = 128
D_FF = 5632
EPS = 1e-6
LOG2E = 1.4426950408889634

F32 = jnp.float32
BF16 = jnp.bfloat16

SUBLANES = 8
LANES = 128
BF16_ROWS = 16

PAIR_LANES = 2 * HEAD_DIM
KV_DUP_WIDTH = N_KV_HEADS * PAIR_LANES
QKV_EXT_WIDTH = D_MODEL + 2 * KV_DUP_WIDTH

VMEM_LIMIT_BYTES = 56 * 1024 * 1024

FFN_TM = 1024
FFN_TF = 512
FFN_ROW_BLOCKS = 8
FFN_EDGE = BF16_ROWS
POOL_TM = 512
POOL_HALO = SUBLANES
QKV_TM = 1024
QKV_TN = 1024


def _rms(x, g):
    ms = jnp.mean(x * x, axis=-1, keepdims=True)
    return x * lax.rsqrt(ms + EPS) * g


def _dot(a, b):
    return jnp.dot(a, b, preferred_element_type=F32)


def _ffn_kernel(x_ref, xp_ref, xn_ref, g_ref, wug_ref, wuv_ref, cwg_ref, cwv_ref, cbg_ref, cbv_ref, wd_ref, gf_ref,
                o_ref, hb_ref, *, seq, tm, tf, final_norm):
    i = pl.program_id(0)
    j = pl.program_id(1)
    nv = tm // SUBLANES
    lane_chunks = [pl.ds(c * LANES, LANES) for c in range(D_MODEL // LANES)]
    rb = tm // FFN_ROW_BLOCKS
    last = FFN_ROW_BLOCKS - 1

    @pl.when(j == 0)
    def _():
        g = g_ref[...]
        pos0 = (i * tm) % seq
        for lanes in lane_chunks:
            o_ref[:, lanes] = jnp.swapaxes(x_ref[:, lanes].reshape(SUBLANES, nv, LANES), 0, 1).reshape(tm, LANES)
        for r in range(FFN_ROW_BLOCKS):
            rows = pl.ds(r * rb, rb)
            hb_ref[rows, :] = _rms(o_ref[rows, :], g).astype(BF16)
        hp = jnp.where(pos0 == 0, 0.0, _rms(xp_ref[...], g))[POOL_HALO - 1:POOL_HALO]
        hn = jnp.where(pos0 + tm == seq, 0.0, _rms(xn_ref[...], g))[0:1]
        edge = jnp.concatenate([hn, hp, jnp.zeros((FFN_EDGE - 2, D_MODEL), F32)], axis=0)
        hb_ref[pl.ds(tm, FFN_EDGE), :] = edge.astype(BF16)

    sub = lax.broadcasted_iota(jnp.int32, (SUBLANES, tf), 0)

    def up(w_ref):
        parts = [None] * FFN_ROW_BLOCKS
        for r in (0, last) + tuple(range(1, last)):
            size = rb + FFN_EDGE if r == last else rb
            parts[r] = _dot(hb_ref[pl.ds(r * rb, size), :], w_ref[...])
        return jnp.concatenate(parts, axis=0)

    def neighbours(u):
        first_prev = jnp.where(sub == 0, u[tm + 1:tm + 2], pltpu.roll(u[tm - SUBLANES:tm], 1, 0))
        last_next = jnp.where(sub == SUBLANES - 1, u[tm:tm + 1], pltpu.roll(u[0:SUBLANES], SUBLANES - 1, 0))
        u_prev = jnp.concatenate([first_prev, u[:tm - SUBLANES]], axis=0)
        u_next = jnp.concatenate([u[SUBLANES:tm], last_next], axis=0)
        return u_prev, u, u_next

    def conv(taps, rows, cw_ref, cb_ref):
        cw = cw_ref[...]
        return taps[0][rows] * cw[0:1] + taps[1][rows] * cw[1:2] + taps[2][rows] * cw[2:3] + cb_ref[...]

    gate_taps = neighbours(up(wug_ref))
    val_taps = neighbours(up(wuv_ref))
    for r in (0,) + tuple(range(1, last)) + (last,):
        rows = slice(r * rb, (r + 1) * rb)
        gate = conv(gate_taps, rows, cwg_ref, cbg_ref)
        val = conv(val_taps, rows, cwv_ref, cbv_ref)
        act = gate / (1.0 + jnp.exp(-gate)) * val
        o_ref[pl.ds(r * rb, rb), :] += _dot(act.astype(BF16), wd_ref[...])

    is_last = j == pl.num_programs(1) - 1
    if final_norm:
        @pl.when(is_last)
        def _():
            for r in range(FFN_ROW_BLOCKS):
                rows = pl.ds(r * rb, rb)
                o_ref[rows, :] = _rms(o_ref[rows, :], gf_ref[...])

    @pl.when(is_last)
    def _():
        for lanes in lane_chunks:
            o_ref[:, lanes] = jnp.swapaxes(o_ref[:, lanes].reshape(nv, SUBLANES, LANES), 0, 1).reshape(tm, LANES)


def _ffn(x, seq, layer, g, wup, cw, cb, wdown, g_final, final_norm):
    n = x.shape[0]
    tm, tf, halo = FFN_TM, FFN_TF, POOL_HALO
    assert seq % tm == 0 and D_FF % tf == 0
    nf = D_FF // tf
    hpt = tm // halo
    last_halo_block = n // halo - 1
    kern = functools.partial(_ffn_kernel, seq=seq, tm=tm, tf=tf, final_norm=final_norm)
    gate_cols = lambda i, j: (layer, 0, j)
    value_cols = lambda i, j: (layer, 0, nf + j)
    return pl.pallas_call(
        kern,
        grid=(n // tm, nf),
        in_specs=[
            pl.BlockSpec((tm, D_MODEL), lambda i, j: (i, 0), pipeline_mode=pl.Buffered(1)),
            pl.BlockSpec((halo, D_MODEL), lambda i, j: (jnp.maximum(i * hpt - 1, 0), 0)),
            pl.BlockSpec((halo, D_MODEL), lambda i, j: (jnp.minimum((i + 1) * hpt, last_halo_block), 0)),
            pl.BlockSpec((None, 1, D_MODEL), lambda i, j: (layer, 0, 0)),
            pl.BlockSpec((None, D_MODEL, tf), gate_cols),
            pl.BlockSpec((None, D_MODEL, tf), value_cols),
            pl.BlockSpec((None, 3, tf), gate_cols),
            pl.BlockSpec((None, 3, tf), value_cols),
            pl.BlockSpec((None, 1, tf), gate_cols),
            pl.BlockSpec((None, 1, tf), value_cols),
            pl.BlockSpec((None, tf, D_MODEL), lambda i, j: (layer, j, 0)),
            pl.BlockSpec((1, D_MODEL), lambda i, j: (0, 0)),
        ],
        out_specs=pl.BlockSpec((tm, D_MODEL), lambda i, j: (i, 0)),
        out_shape=jax.ShapeDtypeStruct((n, D_MODEL), F32),
        scratch_shapes=[pltpu.VMEM((tm + FFN_EDGE, D_MODEL), BF16)],
        compiler_params=pltpu.CompilerParams(
            dimension_semantics=("arbitrary", "arbitrary"), vmem_limit_bytes=VMEM_LIMIT_BYTES),
        name="conv_ffn",
    )(x, x, x, g, wup, wup, cw, cw, cb, cb, wdown, g_final)


def _pool_kernel(x_ref, xp_ref, xn_ref, g_ref, w_ref, sc_ref, o_ref, hh_ref, *, seq, tm):
    i = pl.program_id(0)
    halo = POOL_HALO
    pos0 = (i * tm) % seq
    g = g_ref[...]
    x = x_ref[...]
    h = _rms(x, g)
    hh_ref[pl.ds(halo, tm), :] = h
    hh_ref[pl.ds(0, halo), :] = jnp.where(pos0 == 0, 0.0, _rms(xp_ref[...], g))
    hh_ref[pl.ds(halo + tm, halo), :] = jnp.where(pos0 + tm == seq, 0.0, _rms(xn_ref[...], g))
    t = pos0 + lax.broadcasted_iota(jnp.int32, (tm, 1), 0)
    for gi, win in enumerate(POOL_WINDOWS):
        half = win // 2
        lanes = pl.ds(gi * POOL_GROUP, POOL_GROUP)
        acc = hh_ref[pl.ds(halo - half, tm + win), lanes]
        k = 1
        while k < win:
            length = acc.shape[0] - k
            acc = acc[:length] + acc[k:]
            k *= 2
        acc = acc[:tm]
        cnt = (jnp.clip(t + half, 0, seq) - jnp.clip(t - half, 0, seq)).astype(F32)
        p = acc / cnt - hh_ref[pl.ds(halo, tm), lanes]
        y = _dot(p.astype(BF16), w_ref[gi]) * sc_ref[:, lanes]
        o_ref[:, lanes] = x_ref[:, lanes] + y


def _pool(x, seq, layer, mixer, g, w, scale):
    n = x.shape[0]
    tm, halo = POOL_TM, POOL_HALO
    hpt = tm // halo
    last_halo_block = n // halo - 1
    kern = functools.partial(_pool_kernel, seq=seq, tm=tm)
    return pl.pallas_call(
        kern,
        grid=(n // tm,),
        in_specs=[
            pl.BlockSpec((tm, D_MODEL), lambda i: (i, 0)),
            pl.BlockSpec((halo, D_MODEL), lambda i: (jnp.maximum(i * hpt - 1, 0), 0)),
            pl.BlockSpec((halo, D_MODEL), lambda i: (jnp.minimum((i + 1) * hpt, last_halo_block), 0)),
            pl.BlockSpec((None, 1, D_MODEL), lambda i: (layer, 0, 0)),
            pl.BlockSpec((None, len(POOL_WINDOWS), POOL_GROUP, POOL_GROUP), lambda i: (mixer, 0, 0, 0)),
            pl.BlockSpec((None, 1, D_MODEL), lambda i: (mixer, 0, 0)),
        ],
        out_specs=pl.BlockSpec((tm, D_MODEL), lambda i: (i, 0)),
        out_shape=jax.ShapeDtypeStruct((n, D_MODEL), F32),
        scratch_shapes=[pltpu.VMEM((tm + 2 * halo, D_MODEL), F32)],
        compiler_params=pltpu.CompilerParams(
            dimension_semantics=("arbitrary",), vmem_limit_bytes=VMEM_LIMIT_BYTES),
        name="pool_mixer",
    )(x, x, x, g, w, scale)


def _qkv_kernel(x_ref, g_ref, w_ref, o_ref, hb_ref):
    @pl.when(pl.program_id(1) == 0)
    def _():
        hb_ref[...] = _rms(x_ref[...], g_ref[...]).astype(BF16)

    o_ref[...] = _dot(hb_ref[...], w_ref[...]).astype(BF16)


def _qkv(x, layer, mixer, g, w_ext):
    n = x.shape[0]
    tm, tn = QKV_TM, QKV_TN
    return pl.pallas_call(
        _qkv_kernel,
        grid=(n // tm, QKV_EXT_WIDTH // tn),
        in_specs=[
            pl.BlockSpec((tm, D_MODEL), lambda i, j: (i, 0)),
            pl.BlockSpec((None, 1, D_MODEL), lambda i, j: (layer, 0, 0)),
            pl.BlockSpec((None, D_MODEL, tn), lambda i, j: (mixer, 0, j)),
        ],
        out_specs=pl.BlockSpec((tm, tn), lambda i, j: (i, j)),
        out_shape=jax.ShapeDtypeStruct((n, QKV_EXT_WIDTH), BF16),
        scratch_shapes=[pltpu.VMEM((tm, D_MODEL), BF16)],
        compiler_params=pltpu.CompilerParams(
            dimension_semantics=("arbitrary", "arbitrary"), vmem_limit_bytes=VMEM_LIMIT_BYTES),
        name="qkv_proj",
    )(x, g, w_ext)


def _attn_kernel(sink_ref, q_ref, kp_ref, kc_ref, kn_ref, vp_ref, vc_ref, vn_ref, bias_ref, wo_ref, x_ref,
                 o_ref, kb_ref, vb_ref, ob_ref, s_ref, e_ref, rd_ref, *, seq, mixer):
    i = pl.program_id(0)
    pos0 = (i * BLOCK) % seq
    kb_ref[pl.ds(0, BLOCK), :] = kp_ref[...]
    kb_ref[pl.ds(BLOCK, BLOCK), :] = kc_ref[...]
    kb_ref[pl.ds(2 * BLOCK, BLOCK), :] = kn_ref[...]
    vb_ref[pl.ds(0, BLOCK), :] = vp_ref[...]
    vb_ref[pl.ds(BLOCK, BLOCK), :] = vc_ref[...]
    vb_ref[pl.ds(2 * BLOCK, BLOCK), :] = vn_ref[...]

    kind = jnp.where(pos0 == 0, 1, jnp.where(pos0 + BLOCK == seq, 2, 0))

    lane = lax.broadcasted_iota(jnp.int32, (3 * BLOCK, PAIR_LANES), 1)
    low_half = lane < HEAD_DIM
    zero = jnp.zeros((3 * BLOCK, PAIR_LANES), BF16)
    pairs_per_kv = HEADS_PER_KV // 2

    def scores(kv):
        kk = kb_ref[:, pl.ds(kv * PAIR_LANES, PAIR_LANES)]
        k_half = (jnp.where(low_half, kk, zero), jnp.where(low_half, zero, kk))
        for pr in range(pairs_per_kv):
            pair = kv * pairs_per_kv + pr
            qp = q_ref[:, pl.ds(pair * PAIR_LANES, PAIR_LANES)]
            for par in range(2):
                s = lax.dot_general(qp, k_half[par], (((1,), (1,)), ((), ())), preferred_element_type=F32)
                s_ref[2 * pair + par] = s * (HEAD_DIM ** -0.5 * LOG2E) + bias_ref[kind, 2 * pair + par]

    def softmax(kv):
        for head in range(kv * HEADS_PER_KV, (kv + 1) * HEADS_PER_KV):
            sk = sink_ref[mixer, head] * LOG2E
            m = jnp.maximum(jnp.max(s_ref[head], axis=-1, keepdims=True), sk)
            e = jnp.exp2(s_ref[head] - m)
            denom = jnp.sum(e, axis=-1, keepdims=True) + jnp.exp2(sk - m)
            e_ref[head] = e.astype(BF16)
            rd_ref[head] = 1.0 / denom

    def values(kv):
        vv = vb_ref[:, pl.ds(kv * PAIR_LANES, PAIR_LANES)]
        v_half = (jnp.where(low_half, vv, zero), jnp.where(low_half, zero, vv))
        for pr in range(pairs_per_kv):
            pair = kv * pairs_per_kv + pr
            o_pair = (_dot(e_ref[2 * pair], v_half[0]) * rd_ref[2 * pair]
                      + _dot(e_ref[2 * pair + 1], v_half[1]) * rd_ref[2 * pair + 1])
            ob_ref[:, pl.ds(pair * PAIR_LANES, PAIR_LANES)] = o_pair.astype(BF16)

    for step in range(N_KV_HEADS + 2):
        if step < N_KV_HEADS:
            scores(step)
        if 0 <= step - 1 < N_KV_HEADS:
            softmax(step - 1)
        if 0 <= step - 2 < N_KV_HEADS:
            values(step - 2)
    o_ref[...] = x_ref[...] + _dot(ob_ref[...], wo_ref[...])


def _attn(x, qkv, seq, mixer, bias, wo, sink):
    n = x.shape[0]
    assert seq % BLOCK == 0 and seq > BLOCK, "a block is the first or the last of its sequence, never both"
    nblk = n // BLOCK
    kcol = D_MODEL // KV_DUP_WIDTH
    vcol = kcol + 1
    prev = lambda i: jnp.maximum(i - 1, 0)
    nxt = lambda i: jnp.minimum(i + 1, nblk - 1)
    kern = functools.partial(_attn_kernel, seq=seq, mixer=mixer)
    return pl.pallas_call(
        kern,
        grid=(nblk,),
        in_specs=[
            pl.BlockSpec(memory_space=pltpu.SMEM),
            pl.BlockSpec((BLOCK, D_MODEL), lambda i: (i, 0)),
            pl.BlockSpec((BLOCK, KV_DUP_WIDTH), lambda i: (prev(i), kcol)),
            pl.BlockSpec((BLOCK, KV_DUP_WIDTH), lambda i: (i, kcol)),
            pl.BlockSpec((BLOCK, KV_DUP_WIDTH), lambda i: (nxt(i), kcol)),
            pl.BlockSpec((BLOCK, KV_DUP_WIDTH), lambda i: (prev(i), vcol)),
            pl.BlockSpec((BLOCK, KV_DUP_WIDTH), lambda i: (i, vcol)),
            pl.BlockSpec((BLOCK, KV_DUP_WIDTH), lambda i: (nxt(i), vcol)),
            pl.BlockSpec((3, N_HEADS, BLOCK, 3 * BLOCK), lambda i: (0, 0, 0, 0)),
            pl.BlockSpec((None, D_MODEL, D_MODEL), lambda i: (mixer, 0, 0)),
            pl.BlockSpec((BLOCK, D_MODEL), lambda i: (i, 0)),
        ],
        out_specs=pl.BlockSpec((BLOCK, D_MODEL), lambda i: (i, 0)),
        out_shape=jax.ShapeDtypeStruct((n, D_MODEL), F32),
        scratch_shapes=[
            pltpu.VMEM((3 * BLOCK, KV_DUP_WIDTH), BF16),
            pltpu.VMEM((3 * BLOCK, KV_DUP_WIDTH), BF16),
            pltpu.VMEM((BLOCK, D_MODEL), BF16),
            pltpu.VMEM((N_HEADS, BLOCK, 3 * BLOCK), F32),
            pltpu.VMEM((N_HEADS, BLOCK, 3 * BLOCK), BF16),
            pltpu.VMEM((N_HEADS, BLOCK, 1), F32),
        ],
        compiler_params=pltpu.CompilerParams(
            dimension_semantics=("arbitrary",), vmem_limit_bytes=VMEM_LIMIT_BYTES),
        name="window_attn",
    )(sink, qkv, qkv, qkv, qkv, qkv, qkv, qkv, bias, wo, x)


def _alibi_bias():
    h = np.arange(1, N_HEADS + 1, dtype=np.float32)
    slopes = jnp.asarray(2.0 ** (-8.0 * h / N_HEADS), dtype=F32)
    qi = jnp.arange(BLOCK)[:, None]
    kc = jnp.arange(3 * BLOCK)[None, :]
    dist = jnp.abs(kc - BLOCK - qi)
    bias = -slopes[:, None, None] * dist.astype(F32)[None] * LOG2E
    in_window = dist <= WINDOW
    valid = jnp.stack([in_window, in_window & (kc >= BLOCK), in_window & (kc < 2 * BLOCK)])
    return jnp.where(valid[:, None], bias[None], -jnp.inf)


def _extend_qkv_weight(w_qkv):
    lead = w_qkv.shape[:-1]
    kv_width = N_KV_HEADS * HEAD_DIM
    wq = w_qkv[..., :D_MODEL]
    wk = w_qkv[..., D_MODEL:D_MODEL + kv_width]
    wv = w_qkv[..., D_MODEL + kv_width:]

    def dup(w):
        w = w.reshape(*lead, N_KV_HEADS, 1, HEAD_DIM)
        return jnp.broadcast_to(w, (*lead, N_KV_HEADS, 2, HEAD_DIM)).reshape(*lead, KV_DUP_WIDTH)

    return jnp.concatenate([wq.astype(BF16), dup(wk.astype(BF16)), dup(wv.astype(BF16))], axis=-1)


def _trunk(x3, params):
    b, s, d = x3.shape
    x = x3.reshape(b * s, d)
    for i in range(DEPTH):
        jm = i // 2
        if i % 2 == 0:
            x = _pool(x, s, i, jm, params["norm_mix"], params["pool_w"], params["pool_scale"])
        else:
            qkv = _qkv(x, i, jm, params["norm_mix"], params["wqkv_ext"])
            x = _attn(x, qkv, s, jm, params["bias"], params["attn_wo"], params["attn_sink"])
        x = _ffn(x, s, i, params["norm_ffn"], params["ffn_wup"], params["ffn_conv_w"], params["ffn_conv_b"],
                 params["ffn_wdown"], params["norm_final"], final_norm=(i == DEPTH - 1))
    return x.reshape(b, s, d)


def _prepare_params(norm_mix, norm_ffn, norm_final, pool_w, pool_scale, attn_wqkv, attn_wo, attn_sink,
                    ffn_wup, ffn_conv_w, ffn_conv_b, ffn_wdown):
    return {
        "norm_mix": norm_mix[:, None, :],
        "norm_ffn": norm_ffn[:, None, :],
        "norm_final": norm_final[None, :],
        "pool_w": pool_w.astype(BF16),
        "pool_scale": pool_scale[:, None, :],
        "wqkv_ext": _extend_qkv_weight(attn_wqkv),
        "attn_wo": attn_wo.astype(BF16),
        "attn_sink": attn_sink,
        "bias": _alibi_bias(),
        "ffn_wup": ffn_wup.astype(BF16),
        "ffn_conv_w": ffn_conv_w,
        "ffn_conv_b": ffn_conv_b[:, None, :],
        "ffn_wdown": ffn_wdown.astype(BF16),
    }


def kernel(x_prompt, x_sample, norm_mix, norm_ffn, norm_final, pool_w, pool_scale, attn_wqkv, attn_wo,
           attn_sink, ffn_wup, ffn_conv_w, ffn_conv_b, ffn_wdown):
    params = _prepare_params(norm_mix, norm_ffn, norm_final, pool_w, pool_scale, attn_wqkv, attn_wo,
                             attn_sink, ffn_wup, ffn_conv_w, ffn_conv_b, ffn_wdown)
    return (_trunk(x_prompt, params), _trunk(x_sample, params))
```

```python
import functools

import numpy as np
import jax
import jax.numpy as jnp
from jax import lax
from jax.experimental import pallas as pl
from jax.experimental.pallas import tpu as pltpu

D_MODEL = 2048
DEPTH = 4
POOL_WINDOWS = (2, 4, 8, 16)
POOL_GROUP = D_MODEL // len(POOL_WINDOWS)
HEAD_DIM = 64
N_HEADS = D_MODEL // HEAD_DIM
N_KV_HEADS = N_HEADS // 8
HEADS_PER_KV = N_HEADS // N_KV_HEADS
WINDOW = 128
BLOCK = 128
D_FF = 5632
EPS = 1e-6
LOG2E = 1.4426950408889634

F32 = jnp.float32
BF16 = jnp.bfloat16

SUBLANES = 8
LANES = 128
BF16_ROWS = 16

PAIR_LANES = 2 * HEAD_DIM
KV_DUP_WIDTH = N_KV_HEADS * PAIR_LANES
QKV_EXT_WIDTH = D_MODEL + 2 * KV_DUP_WIDTH

VMEM_LIMIT_BYTES = 56 * 1024 * 1024

FFN_TM = 1024
FFN_TF = 512
FFN_ROW_BLOCKS = 4
FFN_EDGE = BF16_ROWS
POOL_TM = 1024
POOL_HALO = SUBLANES
QKV_TM = 512
QKV_TN = QKV_EXT_WIDTH


def _rms(x, g):
    ms = jnp.mean(x * x, axis=-1, keepdims=True)
    return x * lax.rsqrt(ms + EPS) * g


def _dot(a, b):
    return jnp.dot(a, b, preferred_element_type=F32)


def _ffn_kernel(x_ref, xp_ref, xn_ref, g_ref, wug_ref, wuv_ref, cwg_ref, cwv_ref, cbg_ref, cbv_ref, wd_ref, gf_ref,
                o_ref, hb_ref, *, seq, tm, tf, final_norm):
    i = pl.program_id(0)
    j = pl.program_id(1)
    nv = tm // SUBLANES
    lane_chunks = [pl.ds(c * LANES, LANES) for c in range(D_MODEL // LANES)]
    rb = tm // FFN_ROW_BLOCKS
    last = FFN_ROW_BLOCKS - 1

    @pl.when(j == 0)
    def _():
        g = g_ref[...]
        pos0 = (i * tm) % seq
        for lanes in lane_chunks:
            o_ref[:, lanes] = jnp.swapaxes(x_ref[:, lanes].reshape(SUBLANES, nv, LANES), 0, 1).reshape(tm, LANES)
        for r in range(FFN_ROW_BLOCKS):
            rows = pl.ds(r * rb, rb)
            hb_ref[rows, :] = _rms(o_ref[rows, :], g).astype(BF16)
        hp = jnp.where(pos0 == 0, 0.0, _rms(xp_ref[...], g))[POOL_HALO - 1:POOL_HALO]
        hn = jnp.where(pos0 + tm == seq, 0.0, _rms(xn_ref[...], g))[0:1]
        edge = jnp.concatenate([hn, hp, jnp.zeros((FFN_EDGE - 2, D_MODEL), F32)], axis=0)
        hb_ref[pl.ds(tm, FFN_EDGE), :] = edge.astype(BF16)

    sub = lax.broadcasted_iota(jnp.int32, (SUBLANES, tf), 0)

    def up(w_ref):
        parts = [None] * FFN_ROW_BLOCKS
        for r in (0, last) + tuple(range(1, last)):
            size = rb + FFN_EDGE if r == last else rb
            parts[r] = _dot(hb_ref[pl.ds(r * rb, size), :], w_ref[...])
        return jnp.concatenate(parts, axis=0)

    def neighbours(u):
        first_prev = jnp.where(sub == 0, u[tm + 1:tm + 2], pltpu.roll(u[tm - SUBLANES:tm], 1, 0))
        last_next = jnp.where(sub == SUBLANES - 1, u[tm:tm + 1], pltpu.roll(u[0:SUBLANES], SUBLANES - 1, 0))
        u_prev = jnp.concatenate([first_prev, u[:tm - SUBLANES]], axis=0)
        u_next = jnp.concatenate([u[SUBLANES:tm], last_next], axis=0)
        return u_prev, u, u_next

    def conv(taps, rows, cw_ref, cb_ref):
        cw = cw_ref[...]
        return taps[0][rows] * cw[0:1] + taps[1][rows] * cw[1:2] + taps[2][rows] * cw[2:3] + cb_ref[...]

    gate_taps = neighbours(up(wug_ref))
    val_taps = neighbours(up(wuv_ref))
    for r in (0,) + tuple(range(1, last)) + (last,):
        rows = slice(r * rb, (r + 1) * rb)
        gate = conv(gate_taps, rows, cwg_ref, cbg_ref)
        val = conv(val_taps, rows, cwv_ref, cbv_ref)
        act = gate / (1.0 + jnp.exp(-gate)) * val
        o_ref[pl.ds(r * rb, rb), :] += _dot(act.astype(BF16), wd_ref[...])

    is_last = j == pl.num_programs(1) - 1
    if final_norm:
        @pl.when(is_last)
        def _():
            for r in range(FFN_ROW_BLOCKS):
                rows = pl.ds(r * rb, rb)
                o_ref[rows, :] = _rms(o_ref[rows, :], gf_ref[...])

    @pl.when(is_last)
    def _():
        for lanes in lane_chunks:
            o_ref[:, lanes] = jnp.swapaxes(o_ref[:, lanes].reshape(nv, SUBLANES, LANES), 0, 1).reshape(tm, LANES)


def _ffn(x, seq, layer, g, wup, cw, cb, wdown, g_final, final_norm):
    n = x.shape[0]
    tm, tf, halo = FFN_TM, FFN_TF, POOL_HALO
    assert seq % tm == 0 and D_FF % tf == 0
    nf = D_FF // tf
    hpt = tm // halo
    last_halo_block = n // halo - 1
    kern = functools.partial(_ffn_kernel, seq=seq, tm=tm, tf=tf, final_norm=final_norm)
    gate_cols = lambda i, j: (layer, 0, j)
    value_cols = lambda i, j: (layer, 0, nf + j)
    return pl.pallas_call(
        kern,
        grid=(n // tm, nf),
        in_specs=[
            pl.BlockSpec((tm, D_MODEL), lambda i, j: (i, 0), pipeline_mode=pl.Buffered(1)),
            pl.BlockSpec((halo, D_MODEL), lambda i, j: (jnp.maximum(i * hpt - 1, 0), 0)),
            pl.BlockSpec((halo, D_MODEL), lambda i, j: (jnp.minimum((i + 1) * hpt, last_halo_block), 0)),
            pl.BlockSpec((None, 1, D_MODEL), lambda i, j: (layer, 0, 0)),
            pl.BlockSpec((None, D_MODEL, tf), gate_cols),
            pl.BlockSpec((None, D_MODEL, tf), value_cols),
            pl.BlockSpec((None, 3, tf), gate_cols),
            pl.BlockSpec((None, 3, tf), value_cols),
            pl.BlockSpec((None, 1, tf), gate_cols),
            pl.BlockSpec((None, 1, tf), value_cols),
            pl.BlockSpec((None, tf, D_MODEL), lambda i, j: (layer, j, 0)),
            pl.BlockSpec((1, D_MODEL), lambda i, j: (0, 0)),
        ],
        out_specs=pl.BlockSpec((tm, D_MODEL), lambda i, j: (i, 0)),
        out_shape=jax.ShapeDtypeStruct((n, D_MODEL), F32),
        scratch_shapes=[pltpu.VMEM((tm + FFN_EDGE, D_MODEL), BF16)],
        compiler_params=pltpu.CompilerParams(
            dimension_semantics=("arbitrary", "arbitrary"), vmem_limit_bytes=VMEM_LIMIT_BYTES),
        name="conv_ffn",
    )(x, x, x, g, wup, wup, cw, cw, cb, cb, wdown, g_final)


def _pool_kernel(x_ref, xp_ref, xn_ref, g_ref, w_ref, sc_ref, o_ref, hh_ref, *, seq, tm):
    i = pl.program_id(0)
    halo = POOL_HALO
    pos0 = (i * tm) % seq
    g = g_ref[...]
    x = x_ref[...]
    h = _rms(x, g)
    hh_ref[pl.ds(halo, tm), :] = h
    hh_ref[pl.ds(0, halo), :] = jnp.where(pos0 == 0, 0.0, _rms(xp_ref[...], g))
    hh_ref[pl.ds(halo + tm, halo), :] = jnp.where(pos0 + tm == seq, 0.0, _rms(xn_ref[...], g))
    t = pos0 + lax.broadcasted_iota(jnp.int32, (tm, 1), 0)
    for gi, win in enumerate(POOL_WINDOWS):
        half = win // 2
        lanes = pl.ds(gi * POOL_GROUP, POOL_GROUP)
        acc = hh_ref[pl.ds(halo - half, tm + win), lanes]
        k = 1
        while k < win:
            length = acc.shape[0] - k
            acc = acc[:length] + acc[k:]
            k *= 2
        acc = acc[:tm]
        cnt = (jnp.clip(t + half, 0, seq) - jnp.clip(t - half, 0, seq)).astype(F32)
        p = acc / cnt - hh_ref[pl.ds(halo, tm), lanes]
        y = _dot(p.astype(BF16), w_ref[gi]) * sc_ref[:, lanes]
        o_ref[:, lanes] = x_ref[:, lanes] + y


def _pool(x, seq, layer, mixer, g, w, scale):
    n = x.shape[0]
    tm, halo = POOL_TM, POOL_HALO
    hpt = tm // halo
    last_halo_block = n // halo - 1
    kern = functools.partial(_pool_kernel, seq=seq, tm=tm)
    return pl.pallas_call(
        kern,
        grid=(n // tm,),
        in_specs=[
            pl.BlockSpec((tm, D_MODEL), lambda i: (i, 0)),
            pl.BlockSpec((halo, D_MODEL), lambda i: (jnp.maximum(i * hpt - 1, 0), 0)),
            pl.BlockSpec((halo, D_MODEL), lambda i: (jnp.minimum((i + 1) * hpt, last_halo_block), 0)),
            pl.BlockSpec((None, 1, D_MODEL), lambda i: (layer, 0, 0)),
            pl.BlockSpec((None, len(POOL_WINDOWS), POOL_GROUP, POOL_GROUP), lambda i: (mixer, 0, 0, 0)),
            pl.BlockSpec((None, 1, D_MODEL), lambda i: (mixer, 0, 0)),
        ],
        out_specs=pl.BlockSpec((tm, D_MODEL), lambda i: (i, 0)),
        out_shape=jax.ShapeDtypeStruct((n, D_MODEL), F32),
        scratch_shapes=[pltpu.VMEM((tm + 2 * halo, D_MODEL), F32)],
        compiler_params=pltpu.CompilerParams(
            dimension_semantics=("arbitrary",), vmem_limit_bytes=VMEM_LIMIT_BYTES),
        name="pool_mixer",
    )(x, x, x, g, w, scale)


def _qkv_kernel(x_ref, g_ref, w_ref, o_ref, hb_ref):
    @pl.when(pl.program_id(1) == 0)
    def _():
        hb_ref[...] = _rms(x_ref[...], g_ref[...]).astype(BF16)

    o_ref[...] = _dot(hb_ref[...], w_ref[...]).astype(BF16)


def _qkv(x, layer, mixer, g, w_ext):
    n = x.shape[0]
    tm, tn = QKV_TM, QKV_TN
    return pl.pallas_call(
        _qkv_kernel,
        grid=(n // tm, QKV_EXT_WIDTH // tn),
        in_specs=[
            pl.BlockSpec((tm, D_MODEL), lambda i, j: (i, 0)),
            pl.BlockSpec((None, 1, D_MODEL), lambda i, j: (layer, 0, 0)),
            pl.BlockSpec((None, D_MODEL, tn), lambda i, j: (mixer, 0, j)),
        ],
        out_specs=pl.BlockSpec((tm, tn), lambda i, j: (i, j)),
        out_shape=jax.ShapeDtypeStruct((n, QKV_EXT_WIDTH), BF16),
        scratch_shapes=[pltpu.VMEM((tm, D_MODEL), BF16)],
        compiler_params=pltpu.CompilerParams(
            dimension_semantics=("arbitrary", "arbitrary"), vmem_limit_bytes=VMEM_LIMIT_BYTES),
        name="qkv_proj",
    )(x, g, w_ext)


def _attn_kernel(sink_ref, q_ref, kp_ref, kc_ref, kn_ref, vp_ref, vc_ref, vn_ref, bias_ref, wo_ref, x_ref,
                 o_ref, kb_ref, vb_ref, ob_ref, s_ref, e_ref, rd_ref, *, seq, mixer):
    i = pl.program_id(0)
    pos0 = (i * BLOCK) % seq
    kb_ref[pl.ds(0, BLOCK), :] = kp_ref[...]
    kb_ref[pl.ds(BLOCK, BLOCK), :] = kc_ref[...]
    kb_ref[pl.ds(2 * BLOCK, BLOCK), :] = kn_ref[...]
    vb_ref[pl.ds(0, BLOCK), :] = vp_ref[...]
    vb_ref[pl.ds(BLOCK, BLOCK), :] = vc_ref[...]
    vb_ref[pl.ds(2 * BLOCK, BLOCK), :] = vn_ref[...]

    kind = jnp.where(pos0 == 0, 1, jnp.where(pos0 + BLOCK == seq, 2, 0))

    lane = lax.broadcasted_iota(jnp.int32, (3 * BLOCK, PAIR_LANES), 1)
    low_half = lane < HEAD_DIM
    zero = jnp.zeros((3 * BLOCK, PAIR_LANES), BF16)
    pairs_per_kv = HEADS_PER_KV // 2

    def scores(kv):
        kk = kb_ref[:, pl.ds(kv * PAIR_LANES, PAIR_LANES)]
        k_half = (jnp.where(low_half, kk, zero), jnp.where(low_half, zero, kk))
        for pr in range(pairs_per_kv):
            pair = kv * pairs_per_kv + pr
            qp = q_ref[:, pl.ds(pair * PAIR_LANES, PAIR_LANES)]
            for par in range(2):
                s = lax.dot_general(qp, k_half[par], (((1,), (1,)), ((), ())), preferred_element_type=F32)
                s_ref[2 * pair + par] = s * (HEAD_DIM ** -0.5 * LOG2E) + bias_ref[kind, 2 * pair + par]

    def softmax(kv):
        for head in range(kv * HEADS_PER_KV, (kv + 1) * HEADS_PER_KV):
            sk = sink_ref[mixer, head] * LOG2E
            m = jnp.maximum(jnp.max(s_ref[head], axis=-1, keepdims=True), sk)
            e = jnp.exp2(s_ref[head] - m)
            denom = jnp.sum(e, axis=-1, keepdims=True) + jnp.exp2(sk - m)
            e_ref[head] = e.astype(BF16)
            rd_ref[head] = 1.0 / denom

    def values(kv):
        vv = vb_ref[:, pl.ds(kv * PAIR_LANES, PAIR_LANES)]
        v_half = (jnp.where(low_half, vv, zero), jnp.where(low_half, zero, vv))
        for pr in range(pairs_per_kv):
            pair = kv * pairs_per_kv + pr
            o_pair = (_dot(e_ref[2 * pair], v_half[0]) * rd_ref[2 * pair]
                      + _dot(e_ref[2 * pair + 1], v_half[1]) * rd_ref[2 * pair + 1])
            ob_ref[:, pl.ds(pair * PAIR_LANES, PAIR_LANES)] = o_pair.astype(BF16)

    for kv in range(N_KV_HEADS):
        scores(kv)
    for kv in range(N_KV_HEADS):
        softmax(kv)
    for kv in range(N_KV_HEADS):
        values(kv)
    o_ref[...] = x_ref[...] + _dot(ob_ref[...], wo_ref[...])


def _attn(x, qkv, seq, mixer, bias, wo, sink):
    n = x.shape[0]
    assert seq % BLOCK == 0 and seq > BLOCK, "a block is the first or the last of its sequence, never both"
    nblk = n // BLOCK
    kcol = D_MODEL // KV_DUP_WIDTH
    vcol = kcol + 1
    prev = lambda i: jnp.maximum(i - 1, 0)
    nxt = lambda i: jnp.minimum(i + 1, nblk - 1)
    kern = functools.partial(_attn_kernel, seq=seq, mixer=mixer)
    return pl.pallas_call(
        kern,
        grid=(nblk,),
        in_specs=[
            pl.BlockSpec(memory_space=pltpu.SMEM),
            pl.BlockSpec((BLOCK, D_MODEL), lambda i: (i, 0)),
            pl.BlockSpec((BLOCK, KV_DUP_WIDTH), lambda i: (prev(i), kcol)),
            pl.BlockSpec((BLOCK, KV_DUP_WIDTH), lambda i: (i, kcol)),
            pl.BlockSpec((BLOCK, KV_DUP_WIDTH), lambda i: (nxt(i), kcol)),
            pl.BlockSpec((BLOCK, KV_DUP_WIDTH), lambda i: (prev(i), vcol)),
            pl.BlockSpec((BLOCK, KV_DUP_WIDTH), lambda i: (i, vcol)),
            pl.BlockSpec((BLOCK, KV_DUP_WIDTH), lambda i: (nxt(i), vcol)),
            pl.BlockSpec((3, N_HEADS, BLOCK, 3 * BLOCK), lambda i: (0, 0, 0, 0)),
            pl.BlockSpec((None, D_MODEL, D_MODEL), lambda i: (mixer, 0, 0)),
            pl.BlockSpec((BLOCK, D_MODEL), lambda i: (i, 0)),
        ],
        out_specs=pl.BlockSpec((BLOCK, D_MODEL), lambda i: (i, 0)),
        out_shape=jax.ShapeDtypeStruct((n, D_MODEL), F32),
        scratch_shapes=[
            pltpu.VMEM((3 * BLOCK, KV_DUP_WIDTH), BF16),
            pltpu.VMEM((3 * BLOCK, KV_DUP_WIDTH), BF16),
            pltpu.VMEM((BLOCK, D_MODEL), BF16),
            pltpu.VMEM((N_HEADS, BLOCK, 3 * BLOCK), F32),
            pltpu.VMEM((N_HEADS, BLOCK, 3 * BLOCK), BF16),
            pltpu.VMEM((N_HEADS, BLOCK, 1), F32),
        ],
        compiler_params=pltpu.CompilerParams(
            dimension_semantics=("arbitrary",), vmem_limit_bytes=VMEM_LIMIT_BYTES),
        name="window_attn",
    )(sink, qkv, qkv, qkv, qkv, qkv, qkv, qkv, bias, wo, x)


def _alibi_bias():
    h = np.arange(1, N_HEADS + 1, dtype=np.float32)
    slopes = jnp.asarray(2.0 ** (-8.0 * h / N_HEADS), dtype=F32)
    qi = jnp.arange(BLOCK)[:, None]
    kc = jnp.arange(3 * BLOCK)[None, :]
    dist = jnp.abs(kc - BLOCK - qi)
    bias = -slopes[:, None, None] * dist.astype(F32)[None] * LOG2E
    in_window = dist <= WINDOW
    valid = jnp.stack([in_window, in_window & (kc >= BLOCK), in_window & (kc < 2 * BLOCK)])
    return jnp.where(valid[:, None], bias[None], -jnp.inf)


def _extend_qkv_weight(w_qkv):
    lead = w_qkv.shape[:-1]
    kv_width = N_KV_HEADS * HEAD_DIM
    wq = w_qkv[..., :D_MODEL]
    wk = w_qkv[..., D_MODEL:D_MODEL + kv_width]
    wv = w_qkv[..., D_MODEL + kv_width:]

    def dup(w):
        w = w.reshape(*lead, N_KV_HEADS, 1, HEAD_DIM)
        return jnp.broadcast_to(w, (*lead, N_KV_HEADS, 2, HEAD_DIM)).reshape(*lead, KV_DUP_WIDTH)

    return jnp.concatenate([wq.astype(BF16), dup(wk.astype(BF16)), dup(wv.astype(BF16))], axis=-1)


def _trunk(x3, params):
    b, s, d = x3.shape
    x = x3.reshape(b * s, d)
    for i in range(DEPTH):
        jm = i // 2
        if i % 2 == 0:
            x = _pool(x, s, i, jm, params["norm_mix"], params["pool_w"], params["pool_scale"])
        else:
            qkv = _qkv(x, i, jm, params["norm_mix"], params["wqkv_ext"])
            x = _attn(x, qkv, s, jm, params["bias"], params["attn_wo"], params["attn_sink"])
        x = _ffn(x, s, i, params["norm_ffn"], params["ffn_wup"], params["ffn_conv_w"], params["ffn_conv_b"],
                 params["ffn_wdown"], params["norm_final"], final_norm=(i == DEPTH - 1))
    return x.reshape(b, s, d)


def _prepare_params(norm_mix, norm_ffn, norm_final, pool_w, pool_scale, attn_wqkv, attn_wo, attn_sink,
                    ffn_wup, ffn_conv_w, ffn_conv_b, ffn_wdown):
    return {
        "norm_mix": norm_mix[:, None, :],
        "norm_ffn": norm_ffn[:, None, :],
        "norm_final": norm_final[None, :],
        "pool_w": pool_w.astype(BF16),
        "pool_scale": pool_scale[:, None, :],
        "wqkv_ext": _extend_qkv_weight(attn_wqkv),
        "attn_wo": attn_wo.astype(BF16),
        "attn_sink": attn_sink,
        "bias": _alibi_bias(),
        "ffn_wup": ffn_wup.astype(BF16),
        "ffn_conv_w": ffn_conv_w,
        "ffn_conv_b": ffn_conv_b[:, None, :],
        "ffn_wdown": ffn_wdown.astype(BF16),
    }


def kernel(x_prompt, x_sample, norm_mix, norm_ffn, norm_final, pool_w, pool_scale, attn_wqkv, attn_wo,
           attn_sink, ffn_wup, ffn_conv_w, ffn_conv_b, ffn_wdown):
    params = _prepare_params(norm_mix, norm_ffn, norm_final, pool_w, pool_scale, attn_wqkv, attn_wo,
                             attn_sink, ffn_wup, ffn_conv_w, ffn_conv_b, ffn_wdown)
    return (_trunk(x_prompt, params), _trunk(x_sample, params))
```

```python
import functools

import numpy as np
import jax
import jax.numpy as jnp
from jax import lax
from jax.experimental import pallas as pl
from jax.experimental.pallas import tpu as pltpu

D_MODEL = 2048
DEPTH = 4
POOL_WINDOWS = (2, 4, 8, 16)
POOL_GROUP = D_MODEL // len(POOL_WINDOWS)
HEAD_DIM = 64
N_HEADS = D_MODEL // HEAD_DIM
N_KV_HEADS = N_HEADS // 8
HEADS_PER_KV = N_HEADS // N_KV_HEADS
WINDOW = 128
BLOCK = 128
D_FF = 5632
EPS = 1e-6
LOG2E = 1.4426950408889634

F32 = jnp.float32
BF16 = jnp.bfloat16

SUBLANES = 8
LANES = 128
BF16_ROWS = 16

PAIR_LANES = 2 * HEAD_DIM
KV_DUP_WIDTH = N_KV_HEADS * PAIR_LANES
QKV_EXT_WIDTH = D_MODEL + 2 * KV_DUP_WIDTH

VMEM_LIMIT_BYTES = 56 * 1024 * 1024

FFN_TM = 1024
FFN_TF = 512
FFN_ROW_BLOCKS = 4
FFN_EDGE = BF16_ROWS
POOL_TM = 1024
POOL_HALO = SUBLANES
QKV_TM = 512
QKV_TN = QKV_EXT_WIDTH


def _rms(x, g):
    ms = jnp.mean(x * x, axis=-1, keepdims=True)
    return x * lax.rsqrt(ms + EPS) * g


def _dot(a, b):
    return jnp.dot(a, b, preferred_element_type=F32)


def _ffn_kernel(x_ref, xp_ref, xn_ref, g_ref, wug_ref, wuv_ref, cwg_ref, cwv_ref, cbg_ref, cbv_ref, wd_ref, gf_ref,
                o_ref, hb_ref, *, seq, tm, tf, final_norm):
    i = pl.program_id(0)
    j = pl.program_id(1)
    nv = tm // SUBLANES
    lane_chunks = [pl.ds(c * LANES, LANES) for c in range(D_MODEL // LANES)]
    rb = tm // FFN_ROW_BLOCKS
    last = FFN_ROW_BLOCKS - 1

    def prologue():
        g = g_ref[...]
        pos0 = (i * tm) % seq
        groups = rb // SUBLANES
        for r in (0, last) + tuple(range(1, last)):
            rows = pl.ds(r * rb, rb)
            for lanes in lane_chunks:
                blk = jnp.stack([x_ref[pl.ds(s * nv + r * groups, groups), lanes] for s in range(SUBLANES)])
                o_ref[rows, lanes] = jnp.swapaxes(blk, 0, 1).reshape(rb, LANES)
            hb_ref[rows, :] = _rms(o_ref[rows, :], g).astype(BF16)
        hp = jnp.where(pos0 == 0, 0.0, _rms(xp_ref[...], g))[POOL_HALO - 1:POOL_HALO]
        hn = jnp.where(pos0 + tm == seq, 0.0, _rms(xn_ref[...], g))[0:1]
        edge = jnp.concatenate([hn, hp, jnp.zeros((FFN_EDGE - 2, D_MODEL), F32)], axis=0)
        hb_ref[pl.ds(tm, FFN_EDGE), :] = edge.astype(BF16)

    sub = lax.broadcasted_iota(jnp.int32, (SUBLANES, tf), 0)

    def up(w_ref):
        parts = [None] * FFN_ROW_BLOCKS
        for r in (0, last) + tuple(range(1, last)):
            size = rb + FFN_EDGE if r == last else rb
            parts[r] = _dot(hb_ref[pl.ds(r * rb, size), :], w_ref[...])
        return jnp.concatenate(parts, axis=0)

    def neighbours(u):
        first_prev = jnp.where(sub == 0, u[tm + 1:tm + 2], pltpu.roll(u[tm - SUBLANES:tm], 1, 0))
        last_next = jnp.where(sub == SUBLANES - 1, u[tm:tm + 1], pltpu.roll(u[0:SUBLANES], SUBLANES - 1, 0))
        u_prev = jnp.concatenate([first_prev, u[:tm - SUBLANES]], axis=0)
        u_next = jnp.concatenate([u[SUBLANES:tm], last_next], axis=0)
        return u_prev, u, u_next

    def conv(taps, rows, cw_ref, cb_ref):
        cw = cw_ref[...]
        return taps[0][rows] * cw[0:1] + taps[1][rows] * cw[1:2] + taps[2][rows] * cw[2:3] + cb_ref[...]

    def body():
        gate_taps = neighbours(up(wug_ref))
        val_taps = neighbours(up(wuv_ref))
        for r in range(FFN_ROW_BLOCKS):
            rows = slice(r * rb, (r + 1) * rb)
            gate = conv(gate_taps, rows, cwg_ref, cbg_ref)
            val = conv(val_taps, rows, cwv_ref, cbv_ref)
            act = gate / (1.0 + jnp.exp(-gate)) * val
            o_ref[pl.ds(r * rb, rb), :] += _dot(act.astype(BF16), wd_ref[...])

    @pl.when(j == 0)
    def _():
        prologue()
        body()

    @pl.when(j > 0)
    def _():
        body()

    is_last = j == pl.num_programs(1) - 1
    if final_norm:
        @pl.when(is_last)
        def _():
            for r in range(FFN_ROW_BLOCKS):
                rows = pl.ds(r * rb, rb)
                o_ref[rows, :] = _rms(o_ref[rows, :], gf_ref[...])

    @pl.when(is_last)
    def _():
        for lanes in lane_chunks:
            o_ref[:, lanes] = jnp.swapaxes(o_ref[:, lanes].reshape(nv, SUBLANES, LANES), 0, 1).reshape(tm, LANES)


def _ffn(x, seq, layer, g, wup, cw, cb, wdown, g_final, final_norm):
    n = x.shape[0]
    tm, tf, halo = FFN_TM, FFN_TF, POOL_HALO
    assert seq % tm == 0 and D_FF % tf == 0
    nf = D_FF // tf
    hpt = tm // halo
    last_halo_block = n // halo - 1
    kern = functools.partial(_ffn_kernel, seq=seq, tm=tm, tf=tf, final_norm=final_norm)
    gate_cols = lambda i, j: (layer, 0, j)
    value_cols = lambda i, j: (layer, 0, nf + j)
    return pl.pallas_call(
        kern,
        grid=(n // tm, nf),
        in_specs=[
            pl.BlockSpec((tm, D_MODEL), lambda i, j: (i, 0)),
            pl.BlockSpec((halo, D_MODEL), lambda i, j: (jnp.maximum(i * hpt - 1, 0), 0)),
            pl.BlockSpec((halo, D_MODEL), lambda i, j: (jnp.minimum((i + 1) * hpt, last_halo_block), 0)),
            pl.BlockSpec((None, 1, D_MODEL), lambda i, j: (layer, 0, 0)),
            pl.BlockSpec((None, D_MODEL, tf), gate_cols),
            pl.BlockSpec((None, D_MODEL, tf), value_cols),
            pl.BlockSpec((None, 3, tf), gate_cols),
            pl.BlockSpec((None, 3, tf), value_cols),
            pl.BlockSpec((None, 1, tf), gate_cols),
            pl.BlockSpec((None, 1, tf), value_cols),
            pl.BlockSpec((None, tf, D_MODEL), lambda i, j: (layer, j, 0)),
            pl.BlockSpec((1, D_MODEL), lambda i, j: (0, 0)),
        ],
        out_specs=pl.BlockSpec((tm, D_MODEL), lambda i, j: (i, 0)),
        out_shape=jax.ShapeDtypeStruct((n, D_MODEL), F32),
        scratch_shapes=[pltpu.VMEM((tm + FFN_EDGE, D_MODEL), BF16)],
        compiler_params=pltpu.CompilerParams(
            dimension_semantics=("arbitrary", "arbitrary"), vmem_limit_bytes=VMEM_LIMIT_BYTES),
        name="conv_ffn",
    )(x, x, x, g, wup, wup, cw, cw, cb, cb, wdown, g_final)


def _pool_kernel(x_ref, xp_ref, xn_ref, g_ref, w_ref, sc_ref, o_ref, hh_ref, *, seq, tm):
    i = pl.program_id(0)
    halo = POOL_HALO
    pos0 = (i * tm) % seq
    g = g_ref[...]
    x = x_ref[...]
    h = _rms(x, g)
    hh_ref[pl.ds(halo, tm), :] = h
    hh_ref[pl.ds(0, halo), :] = jnp.where(pos0 == 0, 0.0, _rms(xp_ref[...], g))
    hh_ref[pl.ds(halo + tm, halo), :] = jnp.where(pos0 + tm == seq, 0.0, _rms(xn_ref[...], g))
    t = pos0 + lax.broadcasted_iota(jnp.int32, (tm, 1), 0)
    for gi, win in enumerate(POOL_WINDOWS):
        half = win // 2
        lanes = pl.ds(gi * POOL_GROUP, POOL_GROUP)
        acc = hh_ref[pl.ds(halo - half, tm + win), lanes]
        k = 1
        while k < win:
            length = acc.shape[0] - k
            acc = acc[:length] + acc[k:]
            k *= 2
        acc = acc[:tm]
        cnt = (jnp.clip(t + half, 0, seq) - jnp.clip(t - half, 0, seq)).astype(F32)
        p = acc / cnt - hh_ref[pl.ds(halo, tm), lanes]
        y = _dot(p.astype(BF16), w_ref[gi]) * sc_ref[:, lanes]
        o_ref[:, lanes] = x_ref[:, lanes] + y


def _pool(x, seq, layer, mixer, g, w, scale):
    n = x.shape[0]
    tm, halo = POOL_TM, POOL_HALO
    hpt = tm // halo
    last_halo_block = n // halo - 1
    kern = functools.partial(_pool_kernel, seq=seq, tm=tm)
    return pl.pallas_call(
        kern,
        grid=(n // tm,),
        in_specs=[
            pl.BlockSpec((tm, D_MODEL), lambda i: (i, 0)),
            pl.BlockSpec((halo, D_MODEL), lambda i: (jnp.maximum(i * hpt - 1, 0), 0)),
            pl.BlockSpec((halo, D_MODEL), lambda i: (jnp.minimum((i + 1) * hpt, last_halo_block), 0)),
            pl.BlockSpec((None, 1, D_MODEL), lambda i: (layer, 0, 0)),
            pl.BlockSpec((None, len(POOL_WINDOWS), POOL_GROUP, POOL_GROUP), lambda i: (mixer, 0, 0, 0)),
            pl.BlockSpec((None, 1, D_MODEL), lambda i: (mixer, 0, 0)),
        ],
        out_specs=pl.BlockSpec((tm, D_MODEL), lambda i: (i, 0)),
        out_shape=jax.ShapeDtypeStruct((n, D_MODEL), F32),
        scratch_shapes=[pltpu.VMEM((tm + 2 * halo, D_MODEL), F32)],
        compiler_params=pltpu.CompilerParams(
            dimension_semantics=("arbitrary",), vmem_limit_bytes=VMEM_LIMIT_BYTES),
        name="pool_mixer",
    )(x, x, x, g, w, scale)


def _qkv_kernel(x_ref, g_ref, w_ref, o_ref, hb_ref):
    @pl.when(pl.program_id(1) == 0)
    def _():
        hb_ref[...] = _rms(x_ref[...], g_ref[...]).astype(BF16)

    o_ref[...] = _dot(hb_ref[...], w_ref[...]).astype(BF16)


def _qkv(x, layer, mixer, g, w_ext):
    n = x.shape[0]
    tm, tn = QKV_TM, QKV_TN
    return pl.pallas_call(
        _qkv_kernel,
        grid=(n // tm, QKV_EXT_WIDTH // tn),
        in_specs=[
            pl.BlockSpec((tm, D_MODEL), lambda i, j: (i, 0)),
            pl.BlockSpec((None, 1, D_MODEL), lambda i, j: (layer, 0, 0)),
            pl.BlockSpec((None, D_MODEL, tn), lambda i, j: (mixer, 0, j)),
        ],
        out_specs=pl.BlockSpec((tm, tn), lambda i, j: (i, j)),
        out_shape=jax.ShapeDtypeStruct((n, QKV_EXT_WIDTH), BF16),
        scratch_shapes=[pltpu.VMEM((tm, D_MODEL), BF16)],
        compiler_params=pltpu.CompilerParams(
            dimension_semantics=("arbitrary", "arbitrary"), vmem_limit_bytes=VMEM_LIMIT_BYTES),
        name="qkv_proj",
    )(x, g, w_ext)


def _attn_kernel(sink_ref, q_ref, kp_ref, kc_ref, kn_ref, vp_ref, vc_ref, vn_ref, bias_ref, wo_ref, x_ref,
                 o_ref, kb_ref, vb_ref, ob_ref, s_ref, e_ref, rd_ref, *, seq, mixer):
    i = pl.program_id(0)
    pos0 = (i * BLOCK) % seq
    kb_ref[pl.ds(0, BLOCK), :] = kp_ref[...]
    kb_ref[pl.ds(BLOCK, BLOCK), :] = kc_ref[...]
    kb_ref[pl.ds(2 * BLOCK, BLOCK), :] = kn_ref[...]
    vb_ref[pl.ds(0, BLOCK), :] = vp_ref[...]
    vb_ref[pl.ds(BLOCK, BLOCK), :] = vc_ref[...]
    vb_ref[pl.ds(2 * BLOCK, BLOCK), :] = vn_ref[...]

    kind = jnp.where(pos0 == 0, 1, jnp.where(pos0 + BLOCK == seq, 2, 0))

    lane = lax.broadcasted_iota(jnp.int32, (3 * BLOCK, PAIR_LANES), 1)
    low_half = lane < HEAD_DIM
    zero = jnp.zeros((3 * BLOCK, PAIR_LANES), BF16)
    pairs_per_kv = HEADS_PER_KV // 2

    def scores(kv):
        kk = kb_ref[:, pl.ds(kv * PAIR_LANES, PAIR_LANES)]
        k_half = (jnp.where(low_half, kk, zero), jnp.where(low_half, zero, kk))
        for pr in range(pairs_per_kv):
            pair = kv * pairs_per_kv + pr
            qp = q_ref[:, pl.ds(pair * PAIR_LANES, PAIR_LANES)]
            for par in range(2):
                s = lax.dot_general(qp, k_half[par], (((1,), (1,)), ((), ())), preferred_element_type=F32)
                s_ref[2 * pair + par] = s * (HEAD_DIM ** -0.5 * LOG2E) + bias_ref[kind, 2 * pair + par]

    def softmax(kv):
        for head in range(kv * HEADS_PER_KV, (kv + 1) * HEADS_PER_KV):
            sk = sink_ref[mixer, head] * LOG2E
            m = jnp.maximum(jnp.max(s_ref[head], axis=-1, keepdims=True), sk)
            e = jnp.exp2(s_ref[head] - m)
            denom = jnp.sum(e, axis=-1, keepdims=True) + jnp.exp2(sk - m)
            e_ref[head] = e.astype(BF16)
            rd_ref[head] = 1.0 / denom

    def values(kv):
        vv = vb_ref[:, pl.ds(kv * PAIR_LANES, PAIR_LANES)]
        v_half = (jnp.where(low_half, vv, zero), jnp.where(low_half, zero, vv))
        for pr in range(pairs_per_kv):
            pair = kv * pairs_per_kv + pr
            o_pair = (_dot(e_ref[2 * pair], v_half[0]) * rd_ref[2 * pair]
                      + _dot(e_ref[2 * pair + 1], v_half[1]) * rd_ref[2 * pair + 1])
            ob_ref[:, pl.ds(pair * PAIR_LANES, PAIR_LANES)] = o_pair.astype(BF16)

    for kv in range(N_KV_HEADS):
        scores(kv)
    for kv in range(N_KV_HEADS):
        softmax(kv)
    for kv in range(N_KV_HEADS):
        values(kv)
    o_ref[...] = x_ref[...] + _dot(ob_ref[...], wo_ref[...])


def _attn(x, qkv, seq, mixer, bias, wo, sink):
    n = x.shape[0]
    assert seq % BLOCK == 0 and seq > BLOCK, "a block is the first or the last of its sequence, never both"
    nblk = n // BLOCK
    kcol = D_MODEL // KV_DUP_WIDTH
    vcol = kcol + 1
    prev = lambda i: jnp.maximum(i - 1, 0)
    nxt = lambda i: jnp.minimum(i + 1, nblk - 1)
    kern = functools.partial(_attn_kernel, seq=seq, mixer=mixer)
    return pl.pallas_call(
        kern,
        grid=(nblk,),
        in_specs=[
            pl.BlockSpec(memory_space=pltpu.SMEM),
            pl.BlockSpec((BLOCK, D_MODEL), lambda i: (i, 0)),
            pl.BlockSpec((BLOCK, KV_DUP_WIDTH), lambda i: (prev(i), kcol)),
            pl.BlockSpec((BLOCK, KV_DUP_WIDTH), lambda i: (i, kcol)),
            pl.BlockSpec((BLOCK, KV_DUP_WIDTH), lambda i: (nxt(i), kcol)),
            pl.BlockSpec((BLOCK, KV_DUP_WIDTH), lambda i: (prev(i), vcol)),
            pl.BlockSpec((BLOCK, KV_DUP_WIDTH), lambda i: (i, vcol)),
            pl.BlockSpec((BLOCK, KV_DUP_WIDTH), lambda i: (nxt(i), vcol)),
            pl.BlockSpec((3, N_HEADS, BLOCK, 3 * BLOCK), lambda i: (0, 0, 0, 0)),
            pl.BlockSpec((None, D_MODEL, D_MODEL), lambda i: (mixer, 0, 0)),
            pl.BlockSpec((BLOCK, D_MODEL), lambda i: (i, 0)),
        ],
        out_specs=pl.BlockSpec((BLOCK, D_MODEL), lambda i: (i, 0)),
        out_shape=jax.ShapeDtypeStruct((n, D_MODEL), F32),
        scratch_shapes=[
            pltpu.VMEM((3 * BLOCK, KV_DUP_WIDTH), BF16),
            pltpu.VMEM((3 * BLOCK, KV_DUP_WIDTH), BF16),
            pltpu.VMEM((BLOCK, D_MODEL), BF16),
            pltpu.VMEM((N_HEADS, BLOCK, 3 * BLOCK), F32),
            pltpu.VMEM((N_HEADS, BLOCK, 3 * BLOCK), BF16),
            pltpu.VMEM((N_HEADS, BLOCK, 1), F32),
        ],
        compiler_params=pltpu.CompilerParams(
            dimension_semantics=("arbitrary",), vmem_limit_bytes=VMEM_LIMIT_BYTES),
        name="window_attn",
    )(sink, qkv, qkv, qkv, qkv, qkv, qkv, qkv, bias, wo, x)


def _alibi_bias():
    h = np.arange(1, N_HEADS + 1, dtype=np.float32)
    slopes = jnp.asarray(2.0 ** (-8.0 * h / N_HEADS), dtype=F32)
    qi = jnp.arange(BLOCK)[:, None]
    kc = jnp.arange(3 * BLOCK)[None, :]
    dist = jnp.abs(kc - BLOCK - qi)
    bias = -slopes[:, None, None] * dist.astype(F32)[None] * LOG2E
    in_window = dist <= WINDOW
    valid = jnp.stack([in_window, in_window & (kc >= BLOCK), in_window & (kc < 2 * BLOCK)])
    return jnp.where(valid[:, None], bias[None], -jnp.inf)


def _extend_qkv_weight(w_qkv):
    lead = w_qkv.shape[:-1]
    kv_width = N_KV_HEADS * HEAD_DIM
    wq = w_qkv[..., :D_MODEL]
    wk = w_qkv[..., D_MODEL:D_MODEL + kv_width]
    wv = w_qkv[..., D_MODEL + kv_width:]

    def dup(w):
        w = w.reshape(*lead, N_KV_HEADS, 1, HEAD_DIM)
        return jnp.broadcast_to(w, (*lead, N_KV_HEADS, 2, HEAD_DIM)).reshape(*lead, KV_DUP_WIDTH)

    return jnp.concatenate([wq.astype(BF16), dup(wk.astype(BF16)), dup(wv.astype(BF16))], axis=-1)


def _trunk(x3, params):
    b, s, d = x3.shape
    x = x3.reshape(b * s, d)
    for i in range(DEPTH):
        jm = i // 2
        if i % 2 == 0:
            x = _pool(x, s, i, jm, params["norm_mix"], params["pool_w"], params["pool_scale"])
        else:
            qkv = _qkv(x, i, jm, params["norm_mix"], params["wqkv_ext"])
            x = _attn(x, qkv, s, jm, params["bias"], params["attn_wo"], params["attn_sink"])
        x = _ffn(x, s, i, params["norm_ffn"], params["ffn_wup"], params["ffn_conv_w"], params["ffn_conv_b"],
                 params["ffn_wdown"], params["norm_final"], final_norm=(i == DEPTH - 1))
    return x.reshape(b, s, d)


def _prepare_params(norm_mix, norm_ffn, norm_final, pool_w, pool_scale, attn_wqkv, attn_wo, attn_sink,
                    ffn_wup, ffn_conv_w, ffn_conv_b, ffn_wdown):
    return {
        "norm_mix": norm_mix[:, None, :],
        "norm_ffn": norm_ffn[:, None, :],
        "norm_final": norm_final[None, :],
        "pool_w": pool_w.astype(BF16),
        "pool_scale": pool_scale[:, None, :],
        "wqkv_ext": _extend_qkv_weight(attn_wqkv),
        "attn_wo": attn_wo.astype(BF16),
        "attn_sink": attn_sink,
        "bias": _alibi_bias(),
        "ffn_wup": ffn_wup.astype(BF16),
        "ffn_conv_w": ffn_conv_w,
        "ffn_conv_b": ffn_conv_b[:, None, :],
        "ffn_wdown": ffn_wdown.astype(BF16),
    }


def kernel(x_prompt, x_sample, norm_mix, norm_ffn, norm_final, pool_w, pool_scale, attn_wqkv, attn_wo,
           attn_sink, ffn_wup, ffn_conv_w, ffn_conv_b, ffn_wdown):
    params = _prepare_params(norm_mix, norm_ffn, norm_final, pool_w, pool_scale, attn_wqkv, attn_wo,
                             attn_sink, ffn_wup, ffn_conv_w, ffn_conv_b, ffn_wdown)
    return (_trunk(x_prompt, params), _trunk(x_sample, params))
```

```python
import functools

import numpy as np
import jax
import jax.numpy as jnp
from jax import lax
from jax.experimental import pallas as pl
from jax.experimental.pallas import tpu as pltpu

D_MODEL = 2048
DEPTH = 4
POOL_WINDOWS = (2, 4, 8, 16)
POOL_GROUP = D_MODEL // len(POOL_WINDOWS)
HEAD_DIM = 64
N_HEADS = D_MODEL // HEAD_DIM
N_KV_HEADS = N_HEADS // 8
HEADS_PER_KV = N_HEADS // N_KV_HEADS
WINDOW = 128
BLOCK = 128
D_FF = 5632
EPS = 1e-6
LOG2E = 1.4426950408889634

F32 = jnp.float32
BF16 = jnp.bfloat16

SUBLANES = 8
LANES = 128
BF16_ROWS = 16

PAIR_LANES = 2 * HEAD_DIM
KV_DUP_WIDTH = N_KV_HEADS * PAIR_LANES
QKV_EXT_WIDTH = D_MODEL + 2 * KV_DUP_WIDTH

VMEM_LIMIT_BYTES = 56 * 1024 * 1024

FFN_TM = 1024
FFN_TF = 512
FFN_ROW_BLOCKS = 4
FFN_EDGE = BF16_ROWS
POOL_TM = 1024
POOL_HALO = SUBLANES
QKV_TM = 512
QKV_TN = QKV_EXT_WIDTH


def _rms(x, g):
    ms = jnp.mean(x * x, axis=-1, keepdims=True)
    return x * lax.rsqrt(ms + EPS) * g


def _dot(a, b):
    return jnp.dot(a, b, preferred_element_type=F32)


def _ffn_kernel(x_ref, xp_ref, xn_ref, g_ref, wug_ref, wuv_ref, cwg_ref, cwv_ref, cbg_ref, cbv_ref, wd_ref, gf_ref,
                o_ref, hb_ref, *, seq, tm, tf, final_norm):
    i = pl.program_id(0)
    j = pl.program_id(1)
    nv = tm // SUBLANES
    lane_chunks = [pl.ds(c * LANES, LANES) for c in range(D_MODEL // LANES)]
    rb = tm // FFN_ROW_BLOCKS
    last = FFN_ROW_BLOCKS - 1

    def prologue():
        g = g_ref[...]
        pos0 = (i * tm) % seq
        groups = rb // SUBLANES
        for r in (0, last) + tuple(range(1, last)):
            rows = pl.ds(r * rb, rb)
            for lanes in lane_chunks:
                blk = jnp.stack([x_ref[pl.ds(s * nv + r * groups, groups), lanes] for s in range(SUBLANES)])
                o_ref[rows, lanes] = jnp.swapaxes(blk, 0, 1).reshape(rb, LANES)
            hb_ref[rows, :] = _rms(o_ref[rows, :], g).astype(BF16)
        hp = jnp.where(pos0 == 0, 0.0, _rms(xp_ref[...], g))[POOL_HALO - 1:POOL_HALO]
        hn = jnp.where(pos0 + tm == seq, 0.0, _rms(xn_ref[...], g))[0:1]
        edge = jnp.concatenate([hn, hp, jnp.zeros((FFN_EDGE - 2, D_MODEL), F32)], axis=0)
        hb_ref[pl.ds(tm, FFN_EDGE), :] = edge.astype(BF16)

    sub = lax.broadcasted_iota(jnp.int32, (SUBLANES, tf), 0)

    def up(w_ref):
        parts = [None] * FFN_ROW_BLOCKS
        for r in (0, last) + tuple(range(1, last)):
            size = rb + FFN_EDGE if r == last else rb
            parts[r] = _dot(hb_ref[pl.ds(r * rb, size), :], w_ref[...])
        return jnp.concatenate(parts, axis=0)

    def neighbours(u):
        first_prev = jnp.where(sub == 0, u[tm + 1:tm + 2], pltpu.roll(u[tm - SUBLANES:tm], 1, 0))
        last_next = jnp.where(sub == SUBLANES - 1, u[tm:tm + 1], pltpu.roll(u[0:SUBLANES], SUBLANES - 1, 0))
        u_prev = jnp.concatenate([first_prev, u[:tm - SUBLANES]], axis=0)
        u_next = jnp.concatenate([u[SUBLANES:tm], last_next], axis=0)
        return u_prev, u, u_next

    def conv(taps, rows, cw_ref, cb_ref):
        cw = cw_ref[...]
        return taps[0][rows] * cw[0:1] + taps[1][rows] * cw[1:2] + taps[2][rows] * cw[2:3] + cb_ref[...]

    def body():
        gate_taps = neighbours(up(wug_ref))
        val_taps = neighbours(up(wuv_ref))
        for r in range(FFN_ROW_BLOCKS):
            rows = slice(r * rb, (r + 1) * rb)
            gate = conv(gate_taps, rows, cwg_ref, cbg_ref)
            val = conv(val_taps, rows, cwv_ref, cbv_ref)
            act = gate / (1.0 + jnp.exp(-gate)) * val
            o_ref[pl.ds(r * rb, rb), :] += _dot(act.astype(BF16), wd_ref[...])

    def unpermute():
        for lanes in lane_chunks:
            o_ref[:, lanes] = jnp.swapaxes(o_ref[:, lanes].reshape(nv, SUBLANES, LANES), 0, 1).reshape(tm, LANES)

    is_last = j == pl.num_programs(1) - 1

    @pl.when(j == 0)
    def _():
        prologue()
        body()

    @pl.when((j > 0) & jnp.logical_not(is_last))
    def _():
        body()

    @pl.when(is_last)
    def _():
        body()
        unpermute()

    if final_norm:
        @pl.when(is_last)
        def _():
            for r in range(FFN_ROW_BLOCKS):
                rows = pl.ds(r * rb, rb)
                o_ref[rows, :] = _rms(o_ref[rows, :], gf_ref[...])


def _ffn(x, seq, layer, g, wup, cw, cb, wdown, g_final, final_norm):
    n = x.shape[0]
    tm, tf, halo = FFN_TM, FFN_TF, POOL_HALO
    assert seq % tm == 0 and D_FF % tf == 0
    nf = D_FF // tf
    hpt = tm // halo
    last_halo_block = n // halo - 1
    kern = functools.partial(_ffn_kernel, seq=seq, tm=tm, tf=tf, final_norm=final_norm)
    gate_cols = lambda i, j: (layer, 0, j)
    value_cols = lambda i, j: (layer, 0, nf + j)
    return pl.pallas_call(
        kern,
        grid=(n // tm, nf),
        in_specs=[
            pl.BlockSpec((tm, D_MODEL), lambda i, j: (i, 0)),
            pl.BlockSpec((halo, D_MODEL), lambda i, j: (jnp.maximum(i * hpt - 1, 0), 0)),
            pl.BlockSpec((halo, D_MODEL), lambda i, j: (jnp.minimum((i + 1) * hpt, last_halo_block), 0)),
            pl.BlockSpec((None, 1, D_MODEL), lambda i, j: (layer, 0, 0)),
            pl.BlockSpec((None, D_MODEL, tf), gate_cols),
            pl.BlockSpec((None, D_MODEL, tf), value_cols),
            pl.BlockSpec((None, 3, tf), gate_cols),
            pl.BlockSpec((None, 3, tf), value_cols),
            pl.BlockSpec((None, 1, tf), gate_cols),
            pl.BlockSpec((None, 1, tf), value_cols),
            pl.BlockSpec((None, tf, D_MODEL), lambda i, j: (layer, j, 0)),
            pl.BlockSpec((1, D_MODEL), lambda i, j: (0, 0)),
        ],
        out_specs=pl.BlockSpec((tm, D_MODEL), lambda i, j: (i, 0)),
        out_shape=jax.ShapeDtypeStruct((n, D_MODEL), F32),
        scratch_shapes=[pltpu.VMEM((tm + FFN_EDGE, D_MODEL), BF16)],
        compiler_params=pltpu.CompilerParams(
            dimension_semantics=("arbitrary", "arbitrary"), vmem_limit_bytes=VMEM_LIMIT_BYTES),
        name="conv_ffn",
    )(x, x, x, g, wup, wup, cw, cw, cb, cb, wdown, g_final)


def _pool_kernel(x_ref, xp_ref, xn_ref, g_ref, w_ref, sc_ref, o_ref, hh_ref, *, seq, tm):
    i = pl.program_id(0)
    halo = POOL_HALO
    pos0 = (i * tm) % seq
    g = g_ref[...]
    x = x_ref[...]
    h = _rms(x, g)
    hh_ref[pl.ds(halo, tm), :] = h
    hh_ref[pl.ds(0, halo), :] = jnp.where(pos0 == 0, 0.0, _rms(xp_ref[...], g))
    hh_ref[pl.ds(halo + tm, halo), :] = jnp.where(pos0 + tm == seq, 0.0, _rms(xn_ref[...], g))
    t = pos0 + lax.broadcasted_iota(jnp.int32, (tm, 1), 0)
    for gi, win in enumerate(POOL_WINDOWS):
        half = win // 2
        lanes = pl.ds(gi * POOL_GROUP, POOL_GROUP)
        acc = hh_ref[pl.ds(halo - half, tm + win), lanes]
        k = 1
        while k < win:
            length = acc.shape[0] - k
            acc = acc[:length] + acc[k:]
            k *= 2
        acc = acc[:tm]
        cnt = (jnp.clip(t + half, 0, seq) - jnp.clip(t - half, 0, seq)).astype(F32)
        p = acc / cnt - hh_ref[pl.ds(halo, tm), lanes]
        y = _dot(p.astype(BF16), w_ref[gi]) * sc_ref[:, lanes]
        o_ref[:, lanes] = x_ref[:, lanes] + y


def _pool(x, seq, layer, mixer, g, w, scale):
    n = x.shape[0]
    tm, halo = POOL_TM, POOL_HALO
    hpt = tm // halo
    last_halo_block = n // halo - 1
    kern = functools.partial(_pool_kernel, seq=seq, tm=tm)
    return pl.pallas_call(
        kern,
        grid=(n // tm,),
        in_specs=[
            pl.BlockSpec((tm, D_MODEL), lambda i: (i, 0)),
            pl.BlockSpec((halo, D_MODEL), lambda i: (jnp.maximum(i * hpt - 1, 0), 0)),
            pl.BlockSpec((halo, D_MODEL), lambda i: (jnp.minimum((i + 1) * hpt, last_halo_block), 0)),
            pl.BlockSpec((None, 1, D_MODEL), lambda i: (layer, 0, 0)),
            pl.BlockSpec((None, len(POOL_WINDOWS), POOL_GROUP, POOL_GROUP), lambda i: (mixer, 0, 0, 0)),
            pl.BlockSpec((None, 1, D_MODEL), lambda i: (mixer, 0, 0)),
        ],
        out_specs=pl.BlockSpec((tm, D_MODEL), lambda i: (i, 0)),
        out_shape=jax.ShapeDtypeStruct((n, D_MODEL), F32),
        scratch_shapes=[pltpu.VMEM((tm + 2 * halo, D_MODEL), F32)],
        compiler_params=pltpu.CompilerParams(
            dimension_semantics=("arbitrary",), vmem_limit_bytes=VMEM_LIMIT_BYTES),
        name="pool_mixer",
    )(x, x, x, g, w, scale)


def _qkv_kernel(x_ref, g_ref, w_ref, o_ref, hb_ref):
    @pl.when(pl.program_id(1) == 0)
    def _():
        hb_ref[...] = _rms(x_ref[...], g_ref[...]).astype(BF16)

    o_ref[...] = _dot(hb_ref[...], w_ref[...]).astype(BF16)


def _qkv(x, layer, mixer, g, w_ext):
    n = x.shape[0]
    tm, tn = QKV_TM, QKV_TN
    return pl.pallas_call(
        _qkv_kernel,
        grid=(n // tm, QKV_EXT_WIDTH // tn),
        in_specs=[
            pl.BlockSpec((tm, D_MODEL), lambda i, j: (i, 0)),
            pl.BlockSpec((None, 1, D_MODEL), lambda i, j: (layer, 0, 0)),
            pl.BlockSpec((None, D_MODEL, tn), lambda i, j: (mixer, 0, j)),
        ],
        out_specs=pl.BlockSpec((tm, tn), lambda i, j: (i, j)),
        out_shape=jax.ShapeDtypeStruct((n, QKV_EXT_WIDTH), BF16),
        scratch_shapes=[pltpu.VMEM((tm, D_MODEL), BF16)],
        compiler_params=pltpu.CompilerParams(
            dimension_semantics=("arbitrary", "arbitrary"), vmem_limit_bytes=VMEM_LIMIT_BYTES),
        name="qkv_proj",
    )(x, g, w_ext)


def _attn_kernel(sink_ref, q_ref, kp_ref, kc_ref, kn_ref, vp_ref, vc_ref, vn_ref, bias_ref, wo_ref, x_ref,
                 o_ref, kb_ref, vb_ref, ob_ref, s_ref, e_ref, rd_ref, *, seq, mixer):
    i = pl.program_id(0)
    pos0 = (i * BLOCK) % seq
    kb_ref[pl.ds(0, BLOCK), :] = kp_ref[...]
    kb_ref[pl.ds(BLOCK, BLOCK), :] = kc_ref[...]
    kb_ref[pl.ds(2 * BLOCK, BLOCK), :] = kn_ref[...]
    vb_ref[pl.ds(0, BLOCK), :] = vp_ref[...]
    vb_ref[pl.ds(BLOCK, BLOCK), :] = vc_ref[...]
    vb_ref[pl.ds(2 * BLOCK, BLOCK), :] = vn_ref[...]

    kind = jnp.where(pos0 == 0, 1, jnp.where(pos0 + BLOCK == seq, 2, 0))

    lane = lax.broadcasted_iota(jnp.int32, (3 * BLOCK, PAIR_LANES), 1)
    low_half = lane < HEAD_DIM
    zero = jnp.zeros((3 * BLOCK, PAIR_LANES), BF16)
    pairs_per_kv = HEADS_PER_KV // 2

    def scores(kv):
        kk = kb_ref[:, pl.ds(kv * PAIR_LANES, PAIR_LANES)]
        k_half = (jnp.where(low_half, kk, zero), jnp.where(low_half, zero, kk))
        for pr in range(pairs_per_kv):
            pair = kv * pairs_per_kv + pr
            qp = q_ref[:, pl.ds(pair * PAIR_LANES, PAIR_LANES)]
            for par in range(2):
                s = lax.dot_general(qp, k_half[par], (((1,), (1,)), ((), ())), preferred_element_type=F32)
                s_ref[2 * pair + par] = s * (HEAD_DIM ** -0.5 * LOG2E) + bias_ref[kind, 2 * pair + par]

    def softmax(kv):
        for head in range(kv * HEADS_PER_KV, (kv + 1) * HEADS_PER_KV):
            sk = sink_ref[mixer, head] * LOG2E
            m = jnp.maximum(jnp.max(s_ref[head], axis=-1, keepdims=True), sk)
            e = jnp.exp2(s_ref[head] - m)
            denom = jnp.sum(e, axis=-1, keepdims=True) + jnp.exp2(sk - m)
            e_ref[head] = e.astype(BF16)
            rd_ref[head] = 1.0 / denom

    def values(kv):
        vv = vb_ref[:, pl.ds(kv * PAIR_LANES, PAIR_LANES)]
        v_half = (jnp.where(low_half, vv, zero), jnp.where(low_half, zero, vv))
        for pr in range(pairs_per_kv):
            pair = kv * pairs_per_kv + pr
            o_pair = (_dot(e_ref[2 * pair], v_half[0]) * rd_ref[2 * pair]
                      + _dot(e_ref[2 * pair + 1], v_half[1]) * rd_ref[2 * pair + 1])
            ob_ref[:, pl.ds(pair * PAIR_LANES, PAIR_LANES)] = o_pair.astype(BF16)

    for kv in range(N_KV_HEADS):
        scores(kv)
    for kv in range(N_KV_HEADS):
        softmax(kv)
    for kv in range(N_KV_HEADS):
        values(kv)
    o_ref[...] = x_ref[...] + _dot(ob_ref[...], wo_ref[...])


def _attn(x, qkv, seq, mixer, bias, wo, sink):
    n = x.shape[0]
    assert seq % BLOCK == 0 and seq > BLOCK, "a block is the first or the last of its sequence, never both"
    nblk = n // BLOCK
    kcol = D_MODEL // KV_DUP_WIDTH
    vcol = kcol + 1
    prev = lambda i: jnp.maximum(i - 1, 0)
    nxt = lambda i: jnp.minimum(i + 1, nblk - 1)
    kern = functools.partial(_attn_kernel, seq=seq, mixer=mixer)
    return pl.pallas_call(
        kern,
        grid=(nblk,),
        in_specs=[
            pl.BlockSpec(memory_space=pltpu.SMEM),
            pl.BlockSpec((BLOCK, D_MODEL), lambda i: (i, 0)),
            pl.BlockSpec((BLOCK, KV_DUP_WIDTH), lambda i: (prev(i), kcol)),
            pl.BlockSpec((BLOCK, KV_DUP_WIDTH), lambda i: (i, kcol)),
            pl.BlockSpec((BLOCK, KV_DUP_WIDTH), lambda i: (nxt(i), kcol)),
            pl.BlockSpec((BLOCK, KV_DUP_WIDTH), lambda i: (prev(i), vcol)),
            pl.BlockSpec((BLOCK, KV_DUP_WIDTH), lambda i: (i, vcol)),
            pl.BlockSpec((BLOCK, KV_DUP_WIDTH), lambda i: (nxt(i), vcol)),
            pl.BlockSpec((3, N_HEADS, BLOCK, 3 * BLOCK), lambda i: (0, 0, 0, 0)),
            pl.BlockSpec((None, D_MODEL, D_MODEL), lambda i: (mixer, 0, 0)),
            pl.BlockSpec((BLOCK, D_MODEL), lambda i: (i, 0)),
        ],
        out_specs=pl.BlockSpec((BLOCK, D_MODEL), lambda i: (i, 0)),
        out_shape=jax.ShapeDtypeStruct((n, D_MODEL), F32),
        scratch_shapes=[
            pltpu.VMEM((3 * BLOCK, KV_DUP_WIDTH), BF16),
            pltpu.VMEM((3 * BLOCK, KV_DUP_WIDTH), BF16),
            pltpu.VMEM((BLOCK, D_MODEL), BF16),
            pltpu.VMEM((N_HEADS, BLOCK, 3 * BLOCK), F32),
            pltpu.VMEM((N_HEADS, BLOCK, 3 * BLOCK), BF16),
            pltpu.VMEM((N_HEADS, BLOCK, 1), F32),
        ],
        compiler_params=pltpu.CompilerParams(
            dimension_semantics=("arbitrary",), vmem_limit_bytes=VMEM_LIMIT_BYTES),
        name="window_attn",
    )(sink, qkv, qkv, qkv, qkv, qkv, qkv, qkv, bias, wo, x)


def _alibi_bias():
    h = np.arange(1, N_HEADS + 1, dtype=np.float32)
    slopes = jnp.asarray(2.0 ** (-8.0 * h / N_HEADS), dtype=F32)
    qi = jnp.arange(BLOCK)[:, None]
    kc = jnp.arange(3 * BLOCK)[None, :]
    dist = jnp.abs(kc - BLOCK - qi)
    bias = -slopes[:, None, None] * dist.astype(F32)[None] * LOG2E
    in_window = dist <= WINDOW
    valid = jnp.stack([in_window, in_window & (kc >= BLOCK), in_window & (kc < 2 * BLOCK)])
    return jnp.where(valid[:, None], bias[None], -jnp.inf)


def _extend_qkv_weight(w_qkv):
    lead = w_qkv.shape[:-1]
    kv_width = N_KV_HEADS * HEAD_DIM
    wq = w_qkv[..., :D_MODEL]
    wk = w_qkv[..., D_MODEL:D_MODEL + kv_width]
    wv = w_qkv[..., D_MODEL + kv_width:]

    def dup(w):
        w = w.reshape(*lead, N_KV_HEADS, 1, HEAD_DIM)
        return jnp.broadcast_to(w, (*lead, N_KV_HEADS, 2, HEAD_DIM)).reshape(*lead, KV_DUP_WIDTH)

    return jnp.concatenate([wq.astype(BF16), dup(wk.astype(BF16)), dup(wv.astype(BF16))], axis=-1)


def _trunk(x3, params):
    b, s, d = x3.shape
    x = x3.reshape(b * s, d)
    for i in range(DEPTH):
        jm = i // 2
        if i % 2 == 0:
            x = _pool(x, s, i, jm, params["norm_mix"], params["pool_w"], params["pool_scale"])
        else:
            qkv = _qkv(x, i, jm, params["norm_mix"], params["wqkv_ext"])
            x = _attn(x, qkv, s, jm, params["bias"], params["attn_wo"], params["attn_sink"])
        x = _ffn(x, s, i, params["norm_ffn"], params["ffn_wup"], params["ffn_conv_w"], params["ffn_conv_b"],
                 params["ffn_wdown"], params["norm_final"], final_norm=(i == DEPTH - 1))
    return x.reshape(b, s, d)


def _prepare_params(norm_mix, norm_ffn, norm_final, pool_w, pool_scale, attn_wqkv, attn_wo, attn_sink,
                    ffn_wup, ffn_conv_w, ffn_conv_b, ffn_wdown):
    return {
        "norm_mix": norm_mix[:, None, :],
        "norm_ffn": norm_ffn[:, None, :],
        "norm_final": norm_final[None, :],
        "pool_w": pool_w.astype(BF16),
        "pool_scale": pool_scale[:, None, :],
        "wqkv_ext": _extend_qkv_weight(attn_wqkv),
        "attn_wo": attn_wo.astype(BF16),
        "attn_sink": attn_sink,
        "bias": _alibi_bias(),
        "ffn_wup": ffn_wup.astype(BF16),
        "ffn_conv_w": ffn_conv_w,
        "ffn_conv_b": ffn_conv_b[:, None, :],
        "ffn_wdown": ffn_wdown.astype(BF16),
    }


def kernel(x_prompt, x_sample, norm_mix, norm_ffn, norm_final, pool_w, pool_scale, attn_wqkv, attn_wo,
           attn_sink, ffn_wup, ffn_conv_w, ffn_conv_b, ffn_wdown):
    params = _prepare_params(norm_mix, norm_ffn, norm_final, pool_w, pool_scale, attn_wqkv, attn_wo,
                             attn_sink, ffn_wup, ffn_conv_w, ffn_conv_b, ffn_wdown)
    return (_trunk(x_prompt, params), _trunk(x_sample, params))
```

```python
import functools

import numpy as np
import jax
import jax.numpy as jnp
from jax import lax
from jax.experimental import pallas as pl
from jax.experimental.pallas import tpu as pltpu

D_MODEL = 2048
DEPTH = 4
POOL_WINDOWS = (2, 4, 8, 16)
POOL_GROUP = D_MODEL // len(POOL_WINDOWS)
HEAD_DIM = 64
N_HEADS = D_MODEL // HEAD_DIM
N_KV_HEADS = N_HEADS // 8
HEADS_PER_KV = N_HEADS // N_KV_HEADS
WINDOW = 128
BLOCK = 128
D_FF = 5632
EPS = 1e-6
LOG2E = 1.4426950408889634

F32 = jnp.float32
BF16 = jnp.bfloat16

SUBLANES = 8
LANES = 128
BF16_ROWS = 16

PAIR_LANES = 2 * HEAD_DIM
KV_DUP_WIDTH = N_KV_HEADS * PAIR_LANES
QKV_EXT_WIDTH = D_MODEL + 2 * KV_DUP_WIDTH

VMEM_LIMIT_BYTES = 56 * 1024 * 1024

FFN_TM = 1024
FFN_TF = 512
FFN_ROW_BLOCKS = 4
FFN_EDGE = BF16_ROWS
POOL_TM = 1024
POOL_HALO = SUBLANES
QKV_TM = 512
QKV_TN = QKV_EXT_WIDTH


def _rms(x, g):
    ms = jnp.mean(x * x, axis=-1, keepdims=True)
    return x * lax.rsqrt(ms + EPS) * g


def _dot(a, b):
    return jnp.dot(a, b, preferred_element_type=F32)


def _ffn_kernel(x_ref, xp_ref, xn_ref, g_ref, wug_ref, wuv_ref, cwg_ref, cwv_ref, cbg_ref, cbv_ref, wd_ref, gf_ref,
                o_ref, hb_ref, *, seq, tm, tf, final_norm):
    i = pl.program_id(0)
    j = pl.program_id(1)
    nv = tm // SUBLANES
    lane_chunks = [pl.ds(c * LANES, LANES) for c in range(D_MODEL // LANES)]
    rb = tm // FFN_ROW_BLOCKS
    last = FFN_ROW_BLOCKS - 1

    def prologue():
        g = g_ref[...]
        pos0 = (i * tm) % seq
        groups = rb // SUBLANES
        for r in (0, last) + tuple(range(1, last)):
            rows = pl.ds(r * rb, rb)
            for lanes in lane_chunks:
                blk = jnp.stack([x_ref[pl.ds(s * nv + r * groups, groups), lanes] for s in range(SUBLANES)])
                o_ref[rows, lanes] = jnp.swapaxes(blk, 0, 1).reshape(rb, LANES)
            hb_ref[rows, :] = _rms(o_ref[rows, :], g).astype(BF16)
        hp = jnp.where(pos0 == 0, 0.0, _rms(xp_ref[...], g))[POOL_HALO - 1:POOL_HALO]
        hn = jnp.where(pos0 + tm == seq, 0.0, _rms(xn_ref[...], g))[0:1]
        edge = jnp.concatenate([hn, hp, jnp.zeros((FFN_EDGE - 2, D_MODEL), F32)], axis=0)
        hb_ref[pl.ds(tm, FFN_EDGE), :] = edge.astype(BF16)

    sub = lax.broadcasted_iota(jnp.int32, (SUBLANES, tf), 0)

    def up(w_ref):
        parts = [None] * FFN_ROW_BLOCKS
        for r in (0, last) + tuple(range(1, last)):
            size = rb + FFN_EDGE if r == last else rb
            parts[r] = _dot(hb_ref[pl.ds(r * rb, size), :], w_ref[...])
        return jnp.concatenate(parts, axis=0)

    def neighbours(u):
        first_prev = jnp.where(sub == 0, u[tm + 1:tm + 2], pltpu.roll(u[tm - SUBLANES:tm], 1, 0))
        last_next = jnp.where(sub == SUBLANES - 1, u[tm:tm + 1], pltpu.roll(u[0:SUBLANES], SUBLANES - 1, 0))
        u_prev = jnp.concatenate([first_prev, u[:tm - SUBLANES]], axis=0)
        u_next = jnp.concatenate([u[SUBLANES:tm], last_next], axis=0)
        return u_prev, u, u_next

    def conv(taps, rows, cw_ref, cb_ref):
        cw = cw_ref[...]
        return taps[0][rows] * cw[0:1] + taps[1][rows] * cw[1:2] + taps[2][rows] * cw[2:3] + cb_ref[...]

    def body():
        gate_taps = neighbours(up(wug_ref))
        val_taps = neighbours(up(wuv_ref))
        for r in range(FFN_ROW_BLOCKS):
            rows = slice(r * rb, (r + 1) * rb)
            gate = conv(gate_taps, rows, cwg_ref, cbg_ref)
            val = conv(val_taps, rows, cwv_ref, cbv_ref)
            act = gate / (1.0 + jnp.exp(-gate)) * val
            o_ref[pl.ds(r * rb, rb), :] += _dot(act.astype(BF16), wd_ref[...])

    def unpermute():
        for lanes in lane_chunks:
            o_ref[:, lanes] = jnp.swapaxes(o_ref[:, lanes].reshape(nv, SUBLANES, LANES), 0, 1).reshape(tm, LANES)

    is_last = j == pl.num_programs(1) - 1

    @pl.when(j == 0)
    def _():
        prologue()
        body()

    @pl.when((j > 0) & jnp.logical_not(is_last))
    def _():
        body()

    @pl.when(is_last)
    def _():
        body()
        unpermute()

    if final_norm:
        @pl.when(is_last)
        def _():
            for r in range(FFN_ROW_BLOCKS):
                rows = pl.ds(r * rb, rb)
                o_ref[rows, :] = _rms(o_ref[rows, :], gf_ref[...])


def _ffn(x, seq, layer, g, wup, cw, cb, wdown, g_final, final_norm):
    n = x.shape[0]
    tm, tf, halo = FFN_TM, FFN_TF, POOL_HALO
    assert seq % tm == 0 and D_FF % tf == 0
    nf = D_FF // tf
    hpt = tm // halo
    last_halo_block = n // halo - 1
    kern = functools.partial(_ffn_kernel, seq=seq, tm=tm, tf=tf, final_norm=final_norm)
    gate_cols = lambda i, j: (layer, 0, j)
    value_cols = lambda i, j: (layer, 0, nf + j)
    return pl.pallas_call(
        kern,
        grid=(n // tm, nf),
        in_specs=[
            pl.BlockSpec((tm, D_MODEL), lambda i, j: (i, 0)),
            pl.BlockSpec((halo, D_MODEL), lambda i, j: (jnp.maximum(i * hpt - 1, 0), 0)),
            pl.BlockSpec((halo, D_MODEL), lambda i, j: (jnp.minimum((i + 1) * hpt, last_halo_block), 0)),
            pl.BlockSpec((None, 1, D_MODEL), lambda i, j: (layer, 0, 0)),
            pl.BlockSpec((None, D_MODEL, tf), gate_cols),
            pl.BlockSpec((None, D_MODEL, tf), value_cols),
            pl.BlockSpec((None, 3, tf), gate_cols),
            pl.BlockSpec((None, 3, tf), value_cols),
            pl.BlockSpec((None, 1, tf), gate_cols),
            pl.BlockSpec((None, 1, tf), value_cols),
            pl.BlockSpec((None, tf, D_MODEL), lambda i, j: (layer, j, 0)),
            pl.BlockSpec((1, D_MODEL), lambda i, j: (0, 0)),
        ],
        out_specs=pl.BlockSpec((tm, D_MODEL), lambda i, j: (i, 0)),
        out_shape=jax.ShapeDtypeStruct((n, D_MODEL), F32),
        scratch_shapes=[pltpu.VMEM((tm + FFN_EDGE, D_MODEL), BF16)],
        compiler_params=pltpu.CompilerParams(
            dimension_semantics=("arbitrary", "arbitrary"), vmem_limit_bytes=VMEM_LIMIT_BYTES),
        name="conv_ffn",
    )(x, x, x, g, wup, wup, cw, cw, cb, cb, wdown, g_final)


def _pool_kernel(x_ref, xp_ref, xn_ref, g_ref, w_ref, sc_ref, o_ref, hh_ref, *, seq, tm):
    i = pl.program_id(0)
    halo = POOL_HALO
    pos0 = (i * tm) % seq
    g = g_ref[...]
    x = x_ref[...]
    h = _rms(x, g)
    hh_ref[pl.ds(halo, tm), :] = h
    hh_ref[pl.ds(0, halo), :] = jnp.where(pos0 == 0, 0.0, _rms(xp_ref[...], g))
    hh_ref[pl.ds(halo + tm, halo), :] = jnp.where(pos0 + tm == seq, 0.0, _rms(xn_ref[...], g))
    t = pos0 + lax.broadcasted_iota(jnp.int32, (tm, 1), 0)
    for gi, win in enumerate(POOL_WINDOWS):
        half = win // 2
        lanes = pl.ds(gi * POOL_GROUP, POOL_GROUP)
        acc = hh_ref[pl.ds(halo - half, tm + win), lanes]
        k = 1
        while k < win:
            length = acc.shape[0] - k
            acc = acc[:length] + acc[k:]
            k *= 2
        acc = acc[:tm]
        cnt = (jnp.clip(t + half, 0, seq) - jnp.clip(t - half, 0, seq)).astype(F32)
        p = acc / cnt - hh_ref[pl.ds(halo, tm), lanes]
        y = _dot(p.astype(BF16), w_ref[gi]) * sc_ref[:, lanes]
        o_ref[:, lanes] = x_ref[:, lanes] + y


def _pool(x, seq, layer, mixer, g, w, scale):
    n = x.shape[0]
    tm, halo = POOL_TM, POOL_HALO
    hpt = tm // halo
    last_halo_block = n // halo - 1
    kern = functools.partial(_pool_kernel, seq=seq, tm=tm)
    return pl.pallas_call(
        kern,
        grid=(n // tm,),
        in_specs=[
            pl.BlockSpec((tm, D_MODEL), lambda i: (i, 0)),
            pl.BlockSpec((halo, D_MODEL), lambda i: (jnp.maximum(i * hpt - 1, 0), 0)),
            pl.BlockSpec((halo, D_MODEL), lambda i: (jnp.minimum((i + 1) * hpt, last_halo_block), 0)),
            pl.BlockSpec((None, 1, D_MODEL), lambda i: (layer, 0, 0)),
            pl.BlockSpec((None, len(POOL_WINDOWS), POOL_GROUP, POOL_GROUP), lambda i: (mixer, 0, 0, 0)),
            pl.BlockSpec((None, 1, D_MODEL), lambda i: (mixer, 0, 0)),
        ],
        out_specs=pl.BlockSpec((tm, D_MODEL), lambda i: (i, 0)),
        out_shape=jax.ShapeDtypeStruct((n, D_MODEL), F32),
        scratch_shapes=[pltpu.VMEM((tm + 2 * halo, D_MODEL), F32)],
        compiler_params=pltpu.CompilerParams(
            dimension_semantics=("arbitrary",), vmem_limit_bytes=VMEM_LIMIT_BYTES),
        name="pool_mixer",
    )(x, x, x, g, w, scale)


def _qkv_kernel(x_ref, g_ref, w_ref, o_ref, hb_ref):
    @pl.when(pl.program_id(1) == 0)
    def _():
        hb_ref[...] = _rms(x_ref[...], g_ref[...]).astype(BF16)

    o_ref[...] = _dot(hb_ref[...], w_ref[...]).astype(BF16)


def _qkv(x, layer, mixer, g, w_ext):
    n = x.shape[0]
    tm, tn = QKV_TM, QKV_TN
    return pl.pallas_call(
        _qkv_kernel,
        grid=(n // tm, QKV_EXT_WIDTH // tn),
        in_specs=[
            pl.BlockSpec((tm, D_MODEL), lambda i, j: (i, 0)),
            pl.BlockSpec((None, 1, D_MODEL), lambda i, j: (layer, 0, 0)),
            pl.BlockSpec((None, D_MODEL, tn), lambda i, j: (mixer, 0, j)),
        ],
        out_specs=pl.BlockSpec((tm, tn), lambda i, j: (i, j)),
        out_shape=jax.ShapeDtypeStruct((n, QKV_EXT_WIDTH), BF16),
        scratch_shapes=[pltpu.VMEM((tm, D_MODEL), BF16)],
        compiler_params=pltpu.CompilerParams(
            dimension_semantics=("arbitrary", "arbitrary"), vmem_limit_bytes=VMEM_LIMIT_BYTES),
        name="qkv_proj",
    )(x, g, w_ext)


def _attn_kernel(sink_ref, q_ref, kp_ref, kc_ref, kn_ref, vp_ref, vc_ref, vn_ref, bias_ref, wo_ref, x_ref,
                 o_ref, kb_ref, vb_ref, ob_ref, s_ref, e_ref, rd_ref, *, seq, mixer):
    i = pl.program_id(0)
    pos0 = (i * BLOCK) % seq
    kb_ref[pl.ds(0, BLOCK), :] = kp_ref[...]
    kb_ref[pl.ds(BLOCK, BLOCK), :] = kc_ref[...]
    kb_ref[pl.ds(2 * BLOCK, BLOCK), :] = kn_ref[...]
    vb_ref[pl.ds(0, BLOCK), :] = vp_ref[...]
    vb_ref[pl.ds(BLOCK, BLOCK), :] = vc_ref[...]
    vb_ref[pl.ds(2 * BLOCK, BLOCK), :] = vn_ref[...]

    kind = jnp.where(pos0 == 0, 1, jnp.where(pos0 + BLOCK == seq, 2, 0))

    lane = lax.broadcasted_iota(jnp.int32, (3 * BLOCK, PAIR_LANES), 1)
    low_half = lane < HEAD_DIM
    zero = jnp.zeros((3 * BLOCK, PAIR_LANES), BF16)
    pairs_per_kv = HEADS_PER_KV // 2

    def scores(kv):
        kk = kb_ref[:, pl.ds(kv * PAIR_LANES, PAIR_LANES)]
        k_half = (jnp.where(low_half, kk, zero), jnp.where(low_half, zero, kk))
        qg = jnp.concatenate([q_ref[:, pl.ds((kv * pairs_per_kv + pr) * PAIR_LANES, PAIR_LANES)]
                              for pr in range(pairs_per_kv)], axis=0)
        for par in range(2):
            s4 = lax.dot_general(qg, k_half[par], (((1,), (1,)), ((), ())), preferred_element_type=F32)
            for pr in range(pairs_per_kv):
                head = 2 * (kv * pairs_per_kv + pr) + par
                s_ref[slot(kv, par, pr)] = (s4[pr * BLOCK:(pr + 1) * BLOCK] * (HEAD_DIM ** -0.5 * LOG2E)
                                            + bias_ref[kind, head])

    def slot(kv, par, pr):
        return kv * HEADS_PER_KV + par * pairs_per_kv + pr

    def softmax(kv):
        for par in range(2):
            for pr in range(pairs_per_kv):
                head = 2 * (kv * pairs_per_kv + pr) + par
                idx = slot(kv, par, pr)
                sk = sink_ref[mixer, head] * LOG2E
                m = jnp.maximum(jnp.max(s_ref[idx], axis=-1, keepdims=True), sk)
                e = jnp.exp2(s_ref[idx] - m)
                denom = jnp.sum(e, axis=-1, keepdims=True) + jnp.exp2(sk - m)
                e_ref[idx] = e.astype(BF16)
                rd_ref[idx] = 1.0 / denom

    def values(kv):
        vv = vb_ref[:, pl.ds(kv * PAIR_LANES, PAIR_LANES)]
        v_half = (jnp.where(low_half, vv, zero), jnp.where(low_half, zero, vv))
        o4 = []
        for par in range(2):
            e4 = e_ref[pl.ds(slot(kv, par, 0), pairs_per_kv)].reshape(pairs_per_kv * BLOCK, 3 * BLOCK)
            o4.append(_dot(e4, v_half[par]))
        for pr in range(pairs_per_kv):
            pair = kv * pairs_per_kv + pr
            rows = slice(pr * BLOCK, (pr + 1) * BLOCK)
            o_pair = o4[0][rows] * rd_ref[slot(kv, 0, pr)] + o4[1][rows] * rd_ref[slot(kv, 1, pr)]
            ob_ref[:, pl.ds(pair * PAIR_LANES, PAIR_LANES)] = o_pair.astype(BF16)

    for kv in range(N_KV_HEADS):
        scores(kv)
    for kv in range(N_KV_HEADS):
        softmax(kv)
    for kv in range(N_KV_HEADS):
        values(kv)
    o_ref[...] = x_ref[...] + _dot(ob_ref[...], wo_ref[...])


def _attn(x, qkv, seq, mixer, bias, wo, sink):
    n = x.shape[0]
    assert seq % BLOCK == 0 and seq > BLOCK, "a block is the first or the last of its sequence, never both"
    nblk = n // BLOCK
    kcol = D_MODEL // KV_DUP_WIDTH
    vcol = kcol + 1
    prev = lambda i: jnp.maximum(i - 1, 0)
    nxt = lambda i: jnp.minimum(i + 1, nblk - 1)
    kern = functools.partial(_attn_kernel, seq=seq, mixer=mixer)
    return pl.pallas_call(
        kern,
        grid=(nblk,),
        in_specs=[
            pl.BlockSpec(memory_space=pltpu.SMEM),
            pl.BlockSpec((BLOCK, D_MODEL), lambda i: (i, 0)),
            pl.BlockSpec((BLOCK, KV_DUP_WIDTH), lambda i: (prev(i), kcol)),
            pl.BlockSpec((BLOCK, KV_DUP_WIDTH), lambda i: (i, kcol)),
            pl.BlockSpec((BLOCK, KV_DUP_WIDTH), lambda i: (nxt(i), kcol)),
            pl.BlockSpec((BLOCK, KV_DUP_WIDTH), lambda i: (prev(i), vcol)),
            pl.BlockSpec((BLOCK, KV_DUP_WIDTH), lambda i: (i, vcol)),
            pl.BlockSpec((BLOCK, KV_DUP_WIDTH), lambda i: (nxt(i), vcol)),
            pl.BlockSpec((3, N_HEADS, BLOCK, 3 * BLOCK), lambda i: (0, 0, 0, 0)),
            pl.BlockSpec((None, D_MODEL, D_MODEL), lambda i: (mixer, 0, 0)),
            pl.BlockSpec((BLOCK, D_MODEL), lambda i: (i, 0)),
        ],
        out_specs=pl.BlockSpec((BLOCK, D_MODEL), lambda i: (i, 0)),
        out_shape=jax.ShapeDtypeStruct((n, D_MODEL), F32),
        scratch_shapes=[
            pltpu.VMEM((3 * BLOCK, KV_DUP_WIDTH), BF16),
            pltpu.VMEM((3 * BLOCK, KV_DUP_WIDTH), BF16),
            pltpu.VMEM((BLOCK, D_MODEL), BF16),
            pltpu.VMEM((N_HEADS, BLOCK, 3 * BLOCK), F32),
            pltpu.VMEM((N_HEADS, BLOCK, 3 * BLOCK), BF16),
            pltpu.VMEM((N_HEADS, BLOCK, 1), F32),
        ],
        compiler_params=pltpu.CompilerParams(
            dimension_semantics=("arbitrary",), vmem_limit_bytes=VMEM_LIMIT_BYTES),
        name="window_attn",
    )(sink, qkv, qkv, qkv, qkv, qkv, qkv, qkv, bias, wo, x)


def _alibi_bias():
    h = np.arange(1, N_HEADS + 1, dtype=np.float32)
    slopes = jnp.asarray(2.0 ** (-8.0 * h / N_HEADS), dtype=F32)
    qi = jnp.arange(BLOCK)[:, None]
    kc = jnp.arange(3 * BLOCK)[None, :]
    dist = jnp.abs(kc - BLOCK - qi)
    bias = -slopes[:, None, None] * dist.astype(F32)[None] * LOG2E
    in_window = dist <= WINDOW
    valid = jnp.stack([in_window, in_window & (kc >= BLOCK), in_window & (kc < 2 * BLOCK)])
    return jnp.where(valid[:, None], bias[None], -jnp.inf)


def _extend_qkv_weight(w_qkv):
    lead = w_qkv.shape[:-1]
    kv_width = N_KV_HEADS * HEAD_DIM
    wq = w_qkv[..., :D_MODEL]
    wk = w_qkv[..., D_MODEL:D_MODEL + kv_width]
    wv = w_qkv[..., D_MODEL + kv_width:]

    def dup(w):
        w = w.reshape(*lead, N_KV_HEADS, 1, HEAD_DIM)
        return jnp.broadcast_to(w, (*lead, N_KV_HEADS, 2, HEAD_DIM)).reshape(*lead, KV_DUP_WIDTH)

    return jnp.concatenate([wq.astype(BF16), dup(wk.astype(BF16)), dup(wv.astype(BF16))], axis=-1)


def _trunk(x3, params):
    b, s, d = x3.shape
    x = x3.reshape(b * s, d)
    for i in range(DEPTH):
        jm = i // 2
        if i % 2 == 0:
            x = _pool(x, s, i, jm, params["norm_mix"], params["pool_w"], params["pool_scale"])
        else:
            qkv = _qkv(x, i, jm, params["norm_mix"], params["wqkv_ext"])
            x = _attn(x, qkv, s, jm, params["bias"], params["attn_wo"], params["attn_sink"])
        x = _ffn(x, s, i, params["norm_ffn"], params["ffn_wup"], params["ffn_conv_w"], params["ffn_conv_b"],
                 params["ffn_wdown"], params["norm_final"], final_norm=(i == DEPTH - 1))
    return x.reshape(b, s, d)


def _prepare_params(norm_mix, norm_ffn, norm_final, pool_w, pool_scale, attn_wqkv, attn_wo, attn_sink,
                    ffn_wup, ffn_conv_w, ffn_conv_b, ffn_wdown):
    return {
        "norm_mix": norm_mix[:, None, :],
        "norm_ffn": norm_ffn[:, None, :],
        "norm_final": norm_final[None, :],
        "pool_w": pool_w.astype(BF16),
        "pool_scale": pool_scale[:, None, :],
        "wqkv_ext": _extend_qkv_weight(attn_wqkv),
        "attn_wo": attn_wo.astype(BF16),
        "attn_sink": attn_sink,
        "bias": _alibi_bias(),
        "ffn_wup": ffn_wup.astype(BF16),
        "ffn_conv_w": ffn_conv_w,
        "ffn_conv_b": ffn_conv_b[:, None, :],
        "ffn_wdown": ffn_wdown.astype(BF16),
    }


def kernel(x_prompt, x_sample, norm_mix, norm_ffn, norm_final, pool_w, pool_scale, attn_wqkv, attn_wo,
           attn_sink, ffn_wup, ffn_conv_w, ffn_conv_b, ffn_wdown):
    params = _prepare_params(norm_mix, norm_ffn, norm_final, pool_w, pool_scale, attn_wqkv, attn_wo,
                             attn_sink, ffn_wup, ffn_conv_w, ffn_conv_b, ffn_wdown)
    return (_trunk(x_prompt, params), _trunk(x_sample, params))
```
